```python
import jax, jax.numpy as jnp
from jax import lax
import numpy as np

D_MODEL = 1024
BATCH = 8
SEQ = 8192
DEPTH = 1

HEAD_DIM = 64
N_Q_HEADS = 16
N_KV_HEADS = 4
GROUP = N_Q_HEADS // N_KV_HEADS
ATTN_WIDTH = N_Q_HEADS * HEAD_DIM
KV_WIDTH = N_KV_HEADS * HEAD_DIM
WINDOW = 128
BLOCK = 128
CONV_CH = D_MODEL
CONV_WIDTH = 31
D_FF = 4 * D_MODEL
N_BUCKETS = 32
MAX_DISTANCE = 128
EPS = 1e-6
NEG = -1e30
Q_END = ATTN_WIDTH
K_END = Q_END + KV_WIDTH
V_END = K_END + KV_WIDTH
GLU_END = V_END + 2 * CONV_CH
IN_WIDTH = GLU_END + 2 * D_MODEL

kernel_name = "hybrid_swa_sink_conformer_gated_block"


def rms_norm(x, g):
    xf = x.astype(jnp.float32)
    y = xf * lax.rsqrt(jnp.mean(xf * xf, axis=-1, keepdims=True) + EPS)
    return (y * g.astype(jnp.float32)).astype(x.dtype)


def layer_norm(x, g, b):
    xf = x.astype(jnp.float32)
    mu = jnp.mean(xf, axis=-1, keepdims=True)
    xc = xf - mu
    var = jnp.mean(xc * xc, axis=-1, keepdims=True)
    y = xc * lax.rsqrt(var + EPS) * g.astype(jnp.float32) + b.astype(jnp.float32)
    return y.astype(x.dtype)


def t5_causal_bucket(dist):
    n = jnp.maximum(dist, 0)
    max_exact = N_BUCKETS // 2
    nf = jnp.maximum(n, 1).astype(jnp.float32)
    large = max_exact + (jnp.log(nf / max_exact) / np.float32(np.log(MAX_DISTANCE / max_exact))
                         * (N_BUCKETS - max_exact)).astype(jnp.int32)
    large = jnp.minimum(large, N_BUCKETS - 1)
    return jnp.where(n < max_exact, n, large)


def band_blocks(t, nb):
    b = t.shape[0]
    tp = jnp.pad(t, ((0, 0), (BLOCK, 0), (0, 0), (0, 0))).reshape(b, nb + 1, BLOCK, t.shape[2], t.shape[3])
    return jnp.concatenate([tp[:, :-1], tp[:, 1:]], axis=2)


def sliding_window_attention(q, k, v, sinks, rel_bias):
    b, s = q.shape[0], q.shape[1]
    nb = s // BLOCK
    qb = q.reshape(b, nb, BLOCK, N_KV_HEADS, GROUP, HEAD_DIM)
    kb = band_blocks(k, nb)
    vb = band_blocks(v, nb)
    scores = jnp.einsum('bnqhgd,bnkhd->bnhgqk', qb, kb,
                        preferred_element_type=jnp.float32)
    qi = jnp.arange(BLOCK, dtype=jnp.int32)[:, None]
    kj = jnp.arange(2 * BLOCK, dtype=jnp.int32)[None, :]
    dist = qi + BLOCK - kj
    bias = rel_bias[t5_causal_bucket(dist)].astype(jnp.float32)
    bias = jnp.transpose(bias, (2, 0, 1)).reshape(N_KV_HEADS, GROUP, BLOCK, 2 * BLOCK)
    scores = scores + bias
    key_pos = jnp.arange(nb, dtype=jnp.int32)[:, None] * BLOCK - BLOCK + kj
    valid = ((dist >= 0) & (dist < WINDOW))[None] & (key_pos >= 0)[:, None, :]
    scores = jnp.where(valid[None, :, None, None], scores, NEG)
    sink = sinks.astype(jnp.float32).reshape(N_KV_HEADS, GROUP)[None, None, :, :, None, None]
    sink = jnp.broadcast_to(sink, scores.shape[:-1] + (1,))
    probs = jax.nn.softmax(jnp.concatenate([scores, sink], axis=-1), axis=-1)[..., :-1]
    o = jnp.einsum('bnhgqk,bnkhd->bnqhgd', probs.astype(v.dtype), vb)
    return o.reshape(b, s, ATTN_WIDTH)


def conformer_conv(glu_in, w_dw, b_dw, ln_g, ln_b, w_conv_out):
    a, gate = jnp.split(glu_in, 2, axis=-1)
    h = a * jax.nn.sigmoid(gate)
    h = lax.conv_general_dilated(
        h, w_dw[:, None, :].astype(h.dtype), window_strides=(1,),
        padding=[(CONV_WIDTH - 1, 0)],
        dimension_numbers=('NWC', 'WIO', 'NWC'),
        feature_group_count=CONV_CH) + b_dw
    h = jax.nn.silu(layer_norm(h, ln_g, ln_b))
    return h @ w_conv_out


def setup_inputs(seed: int = 0) -> dict:
    key = jax.random.key(seed)
    ks = jax.random.split(key, 20)
    f32 = jnp.float32

    def nrm(k, shape, scale):
        return jax.random.normal(k, shape, f32) * scale

    L = DEPTH
    return {
        "x": nrm(ks[0], (BATCH, SEQ, D_MODEL), 1.0),
        "norm_mix_g": 1.0 + nrm(ks[1], (L, D_MODEL), 0.02),
        "w_in": nrm(ks[2], (L, D_MODEL, IN_WIDTH), D_MODEL ** -0.5),
        "q_norm_g": 1.0 + nrm(ks[3], (L, HEAD_DIM), 0.02),
        "k_norm_g": 1.0 + nrm(ks[4], (L, HEAD_DIM), 0.02),
        "attn_sinks": nrm(ks[5], (L, N_Q_HEADS), 0.5),
        "rel_bias": nrm(ks[6], (N_BUCKETS, N_Q_HEADS), 0.5),
        "w_attn_o": nrm(ks[7], (L, ATTN_WIDTH, D_MODEL), ATTN_WIDTH ** -0.5),
        "w_dw": nrm(ks[8], (L, CONV_WIDTH, CONV_CH), CONV_WIDTH ** -0.5),
        "b_dw": nrm(ks[9], (L, CONV_CH), 0.02),
        "conv_ln_g": 1.0 + nrm(ks[10], (L, CONV_CH), 0.02),
        "conv_ln_b": nrm(ks[11], (L, CONV_CH), 0.02),
        "w_conv_out": nrm(ks[12], (L, CONV_CH, D_MODEL), CONV_CH ** -0.5),
        "w_out": nrm(ks[13], (L, D_MODEL, D_MODEL), D_MODEL ** -0.5),
        "norm_mlp_g": 1.0 + nrm(ks[14], (L, D_MODEL), 0.02),
        "w_ff1": nrm(ks[15], (L, D_MODEL, D_FF), D_MODEL ** -0.5),
        "w_ff2": nrm(ks[16], (L, D_FF, D_MODEL), D_FF ** -0.5),
    }


def reference(x, norm_mix_g, w_in, q_norm_g, k_norm_g, attn_sinks, rel_bias, w_attn_o,
              w_dw, b_dw, conv_ln_g, conv_ln_b, w_conv_out, w_out, norm_mlp_g, w_ff1, w_ff2):
    b, s, _ = x.shape
    for l in range(DEPTH):
        u = rms_norm(x, norm_mix_g[l])
        proj = u @ w_in[l]
        q = proj[..., :Q_END].reshape(b, s, N_Q_HEADS, HEAD_DIM)
        k = proj[..., Q_END:K_END].reshape(b, s, N_KV_HEADS, HEAD_DIM)
        v = proj[..., K_END:V_END].reshape(b, s, N_KV_HEADS, HEAD_DIM)
        glu_in = proj[..., V_END:GLU_END]
        gate_attn, gate_conv = jnp.split(proj[..., GLU_END:], 2, axis=-1)

        q = rms_norm(q, q_norm_g[l]) * (HEAD_DIM ** -0.5)
        k = rms_norm(k, k_norm_g[l])
        attn = sliding_window_attention(q, k, v, attn_sinks[l], rel_bias) @ w_attn_o[l]
        conv = conformer_conv(glu_in, w_dw[l], b_dw[l], conv_ln_g[l], conv_ln_b[l], w_conv_out[l])

        merged = jax.nn.sigmoid(gate_attn) * attn + jax.nn.sigmoid(gate_conv) * conv
        x = x + merged @ w_out[l]

        hmid = jnp.square(jax.nn.relu(rms_norm(x, norm_mlp_g[l]) @ w_ff1[l]))
        x = x + hmid @ w_ff2[l]
    return x
```

```python
import functools

import numpy as np
import jax
import jax.numpy as jnp
from jax import lax
from jax.experimental import pallas as pl
from jax.experimental.pallas import tpu as pltpu

D_MODEL = 1024
HEAD_DIM = 64
N_Q_HEADS = 16
N_KV_HEADS = 4
KV_WIDTH = N_KV_HEADS * HEAD_DIM
BLOCK = 128
CONV_WIDTH = 31
D_FF = 4 * D_MODEL
N_BUCKETS = 32
MAX_DISTANCE = 128
EPS = 1e-6
NEG = -1e30

LANES = 128
SUBLANES = 8
N_LANE_BLOCKS = D_MODEL // LANES
CONV_HALO = 32
CONV_SKIP = CONV_HALO - (CONV_WIDTH - 1)
CONV_ROWS = 64

MIX_TILE = 512
MLP_TILE = 512
FF_CHUNK = 1024
VMEM_LIMIT = 56 * 1024 * 1024

F32 = jnp.float32
BF16 = jnp.bfloat16


def _bucket_table():
    qi = np.arange(BLOCK, dtype=np.int32)[:, None]
    kj = np.arange(2 * BLOCK, dtype=np.int32)[None, :]
    dist = qi + BLOCK - kj
    n = np.maximum(dist, 0)
    max_exact = N_BUCKETS // 2
    nf = np.maximum(n, 1).astype(np.float32)
    large = max_exact + (np.log(nf / np.float32(max_exact)) / np.float32(np.log(MAX_DISTANCE / max_exact))
                         * np.float32(N_BUCKETS - max_exact)).astype(np.int32)
    large = np.minimum(large, N_BUCKETS - 1)
    bucket = np.where(n < max_exact, n, large)
    valid = (dist >= 0) & (dist < BLOCK)
    return np.where(valid, bucket, -1).astype(np.int32)


def _head_mean_matrix():
    idx = np.arange(2 * LANES) // HEAD_DIM
    return (idx[:, None] == idx[None, :]).astype(np.float32) / HEAD_DIM


def _dot(a, b):
    return jnp.dot(a, b, preferred_element_type=F32)


def _dot_nt(a, b):
    return lax.dot_general(a, b, (((1,), (1,)), ((), ())), preferred_element_type=F32)


def _head_rms(t, hm, gain, scale):
    sq = (t * t).astype(BF16)
    parts = []
    for c in range(t.shape[1] // (2 * LANES)):
        parts.append(_dot(sq[:, c * 2 * LANES:(c + 1) * 2 * LANES], hm))
    ms = parts[0] if len(parts) == 1 else jnp.concatenate(parts, axis=1)
    return t * lax.rsqrt(ms + EPS) * gain * scale


def _mixer_kernel(x_ref, bucket_ref, relb_ref, sink_ref, hm_ref,
                  g_mix_ref, gq_ref, gk_ref, w_q_ref, w_k_ref, w_v_ref, w_a_ref, w_g_ref,
                  w_ga_ref, w_gc_ref, w_ao_ref, w_dw_ref, b_dw_ref, ln_g_ref, ln_b_ref,
                  w_co_ref, w_out_ref,
                  o_ref,
                  bias_s, q_s, kpad_s, vpad_s, attn_s, h_s, conv_s):
    tile = x_ref.shape[1]
    n_blocks = tile // BLOCK
    b = pl.program_id(0)
    j = pl.program_id(1)

    @pl.when((b == 0) & (j == 0))
    def _build_bias():
        bk = bucket_ref[...]
        left = lax.broadcasted_iota(jnp.int32, bk.shape, 1) < BLOCK
        for h in range(N_Q_HEADS):
            acc = jnp.full(bk.shape, NEG, F32)
            for bb in range(N_BUCKETS):
                acc = jnp.where(bk == bb, relb_ref[bb, h], acc)
            bias_s[0, h] = acc
            bias_s[1, h] = jnp.where(left, NEG, acc)

    @pl.when(j == 0)
    def _zero_halo():
        kpad_s[:, 0:BLOCK, :] = jnp.zeros((2 * N_KV_HEADS, BLOCK, LANES), BF16)
        vpad_s[:, 0:BLOCK, :] = jnp.zeros((2 * N_KV_HEADS, BLOCK, LANES), BF16)
        h_s[:, 0:CONV_HALO, :] = jnp.zeros((N_LANE_BLOCKS, CONV_HALO, LANES), F32)

    @pl.when(j != 0)
    def _carry_halo():
        kpad_s[:, 0:BLOCK, :] = kpad_s[:, tile:tile + BLOCK, :]
        vpad_s[:, 0:BLOCK, :] = vpad_s[:, tile:tile + BLOCK, :]
        h_s[:, 0:CONV_HALO, :] = h_s[:, tile:tile + CONV_HALO, :]

    x = x_ref[0]
    u = (x * lax.rsqrt(jnp.mean(x * x, axis=-1, keepdims=True) + EPS) * g_mix_ref[...]).astype(BF16)

    hm = hm_ref[...]
    q_s[...] = _head_rms(_dot(u, w_q_ref[...]), hm, gq_ref[...], HEAD_DIM ** -0.5).astype(BF16)

    kn = _head_rms(_dot(u, w_k_ref[...]), hm, gk_ref[...], 1.0)
    vv = _dot(u, w_v_ref[...])
    low = lax.broadcasted_iota(jnp.int32, (tile, LANES), 1) < HEAD_DIM
    for src, dst in ((kn, kpad_s), (vv, vpad_s)):
        for c in range(KV_WIDTH // LANES):
            s0 = src[:, c * LANES:(c + 1) * LANES]
            s1 = pltpu.roll(s0, HEAD_DIM, 1)
            dst[4 * c + 0, BLOCK:BLOCK + tile, :] = jnp.where(low, s0, 0.0).astype(BF16)
            dst[4 * c + 1, BLOCK:BLOCK + tile, :] = jnp.where(low, 0.0, s1).astype(BF16)
            dst[4 * c + 2, BLOCK:BLOCK + tile, :] = jnp.where(low, s1, 0.0).astype(BF16)
            dst[4 * c + 3, BLOCK:BLOCK + tile, :] = jnp.where(low, 0.0, s0).astype(BF16)

    def _attn_block(n, carry):
        r0 = pl.multiple_of(n * BLOCK, BLOCK)
        first = jnp.where((j == 0) & (n == 0), 1, 0)
        for p in range(N_Q_HEADS // 2):
            g = p // 2
            q2 = q_s[pl.ds(r0, BLOCK), p * LANES:(p + 1) * LANES]
            o_pair = None
            for i in range(2):
                h = 2 * p + i
                kb = kpad_s[2 * g + i, pl.ds(r0, 2 * BLOCK), :]
                vb = vpad_s[2 * g + i, pl.ds(r0, 2 * BLOCK), :]
                s = _dot_nt(q2, kb) + bias_s[first, h]
                sink = sink_ref[h]
                m = jnp.maximum(jnp.max(s, axis=-1, keepdims=True), sink)
                e = jnp.exp(s - m)
                denom = jnp.sum(e, axis=-1, keepdims=True) + jnp.exp(sink - m)
                o = _dot(e.astype(BF16), vb) * (1.0 / denom)
                o_pair = o if o_pair is None else o_pair + o
            attn_s[pl.ds(r0, BLOCK), p * LANES:(p + 1) * LANES] = o_pair.astype(BF16)
        return carry

    lax.fori_loop(0, n_blocks, _attn_block, 0)

    hglu = _dot(u, w_a_ref[...]) * jax.nn.sigmoid(_dot(u, w_g_ref[...]))
    for c in range(N_LANE_BLOCKS):
        h_s[c, CONV_HALO:CONV_HALO + tile, :] = hglu[:, c * LANES:(c + 1) * LANES]

    def _conv_lanes(c, carry):
        for r in range(tile // CONV_ROWS):
            acc = jnp.broadcast_to(b_dw_ref[c], (CONV_ROWS, LANES))
            for t in range(CONV_WIDTH):
                win = h_s[c, r * CONV_ROWS + CONV_SKIP + t:r * CONV_ROWS + CONV_SKIP + t + CONV_ROWS, :]
                acc = acc + win * w_dw_ref[c, t:t + 1, :]
            conv_s[c, r * CONV_ROWS:(r + 1) * CONV_ROWS, :] = acc
        return carry

    lax.fori_loop(0, N_LANE_BLOCKS, _conv_lanes, 0)

    cv = jnp.concatenate([conv_s[c] for c in range(N_LANE_BLOCKS)], axis=1)
    mu = jnp.mean(cv, axis=-1, keepdims=True)
    xc = cv - mu
    var = jnp.mean(xc * xc, axis=-1, keepdims=True)
    y = xc * lax.rsqrt(var + EPS) * ln_g_ref[...] + ln_b_ref[...]
    conv = _dot((y * jax.nn.sigmoid(y)).astype(BF16), w_co_ref[...])

    attn = _dot(attn_s[...], w_ao_ref[...])
    merged = (jax.nn.sigmoid(_dot(u, w_ga_ref[...])) * attn
              + jax.nn.sigmoid(_dot(u, w_gc_ref[...])) * conv)
    o_ref[0] = x + _dot(merged.astype(BF16), w_out_ref[...])


def _mlp_kernel(x_ref, g_ref, w1_ref, w2_ref, o_ref):
    x = x_ref[...]
    xn = (x * lax.rsqrt(jnp.mean(x * x, axis=-1, keepdims=True) + EPS) * g_ref[...]).astype(BF16)
    acc = x
    for c in range(D_FF // FF_CHUNK):
        hmid = jnp.square(jnp.maximum(_dot(xn, w1_ref[:, c * FF_CHUNK:(c + 1) * FF_CHUNK]), 0.0))
        acc = acc + _dot(hmid.astype(BF16), w2_ref[c * FF_CHUNK:(c + 1) * FF_CHUNK, :])
    o_ref[...] = acc


def _resident(shape):
    zeros = (0,) * len(shape)
    return pl.BlockSpec(shape, lambda *_: zeros, pipeline_mode=pl.Buffered(1))


def _smem():
    return pl.BlockSpec(memory_space=pltpu.SMEM)


def kernel(x, norm_mix_g, w_in, q_norm_g, k_norm_g, attn_sinks, rel_bias, w_attn_o, w_dw, b_dw,
           conv_ln_g, conv_ln_b, w_conv_out, w_out, norm_mlp_g, w_ff1, w_ff2):
    batch, seq, d = x.shape
    assert d == D_MODEL and seq % MIX_TILE == 0 and (batch * seq) % MLP_TILE == 0
    layer = 0
    row = lambda v: v.reshape(1, -1).astype(F32)
    w = w_in[layer].astype(BF16)
    q_end = N_Q_HEADS * HEAD_DIM
    k_end = q_end + KV_WIDTH
    v_end = k_end + KV_WIDTH
    a_end = v_end + D_MODEL
    g_end = a_end + D_MODEL
    ga_end = g_end + D_MODEL
    lane_blocks = lambda v: v.astype(F32).reshape(-1, N_LANE_BLOCKS, LANES).transpose(1, 0, 2)

    operands = [
        x,
        jnp.asarray(_bucket_table()),
        rel_bias.astype(F32),
        attn_sinks[layer].astype(F32),
        jnp.asarray(_head_mean_matrix(), dtype=BF16),
        row(norm_mix_g[layer]),
        row(jnp.tile(q_norm_g[layer], N_Q_HEADS)),
        row(jnp.tile(k_norm_g[layer], N_KV_HEADS)),
        w[:, :q_end], w[:, q_end:k_end], w[:, k_end:v_end], w[:, v_end:a_end], w[:, a_end:g_end],
        w[:, g_end:ga_end], w[:, ga_end:],
        w_attn_o[layer].astype(BF16),
        lane_blocks(w_dw[layer]),
        lane_blocks(b_dw[layer].reshape(1, -1)),
        row(conv_ln_g[layer]), row(conv_ln_b[layer]),
        w_conv_out[layer].astype(BF16),
        w_out[layer].astype(BF16),
    ]
    in_specs = [pl.BlockSpec((1, MIX_TILE, D_MODEL), lambda b, j: (b, j, 0)),
                _resident(operands[1].shape), _smem(), _smem()]
    in_specs += [_resident(op.shape) for op in operands[4:]]

    x1 = pl.pallas_call(
        _mixer_kernel,
        grid=(batch, seq // MIX_TILE),
        in_specs=in_specs,
        out_specs=pl.BlockSpec((1, MIX_TILE, D_MODEL), lambda b, j: (b, j, 0)),
        out_shape=jax.ShapeDtypeStruct(x.shape, F32),
        scratch_shapes=[
            pltpu.VMEM((2, N_Q_HEADS, BLOCK, 2 * BLOCK), F32),
            pltpu.VMEM((MIX_TILE, D_MODEL), BF16),
            pltpu.VMEM((2 * N_KV_HEADS, BLOCK + MIX_TILE, LANES), BF16),
            pltpu.VMEM((2 * N_KV_HEADS, BLOCK + MIX_TILE, LANES), BF16),
            pltpu.VMEM((MIX_TILE, D_MODEL), BF16),
            pltpu.VMEM((N_LANE_BLOCKS, CONV_HALO + MIX_TILE, LANES), F32),
            pltpu.VMEM((N_LANE_BLOCKS, MIX_TILE, LANES), F32),
        ],
        compiler_params=pltpu.CompilerParams(
            dimension_semantics=("arbitrary", "arbitrary"), vmem_limit_bytes=VMEM_LIMIT),
        name="mixer",
    )(*operands)

    tokens = batch * seq
    out = pl.pallas_call(
        _mlp_kernel,
        grid=(tokens // MLP_TILE,),
        in_specs=[pl.BlockSpec((MLP_TILE, D_MODEL), lambda i: (i, 0)),
                  _resident((1, D_MODEL)), _resident((D_MODEL, D_FF)), _resident((D_FF, D_MODEL))],
        out_specs=pl.BlockSpec((MLP_TILE, D_MODEL), lambda i: (i, 0)),
        out_shape=jax.ShapeDtypeStruct((tokens, D_MODEL), F32),
        compiler_params=pltpu.CompilerParams(
            dimension_semantics=("arbitrary",), vmem_limit_bytes=VMEM_LIMIT),
        name="mlp",
    )(x1.reshape(tokens, D_MODEL), row(norm_mlp_g[layer]),
      w_ff1[layer].astype(BF16), w_ff2[layer].astype(BF16))
    return out.reshape(batch, seq, D_MODEL)
```

```python
import functools

import numpy as np
import jax
import jax.numpy as jnp
from jax import lax
from jax.experimental import pallas as pl
from jax.experimental.pallas import tpu as pltpu

D_MODEL = 1024
HEAD_DIM = 64
N_Q_HEADS = 16
N_KV_HEADS = 4
KV_WIDTH = N_KV_HEADS * HEAD_DIM
GROUP = N_Q_HEADS // N_KV_HEADS
BLOCK = 128
CONV_WIDTH = 31
D_FF = 4 * D_MODEL
N_BUCKETS = 32
MAX_DISTANCE = 128
EPS = 1e-6
NEG = -1e30
LOG2E = 1.4426950408889634

LANES = 128
SUBLANES = 8
N_LANE_BLOCKS = D_MODEL // LANES
COL_CHUNK = 2 * LANES
N_COL_CHUNKS = D_MODEL // COL_CHUNK
CONV_HALO = 32
CONV_SKIP = CONV_HALO - (CONV_WIDTH - 1)
CONV_ROWS = 64

MIX_TILE = 512
MLP_TILE = 512
FF_CHUNK = 1024
VMEM_LIMIT = 56 * 1024 * 1024

F32 = jnp.float32
BF16 = jnp.bfloat16


def _bucket_table():
    qi = np.arange(BLOCK, dtype=np.int32)[:, None]
    kj = np.arange(2 * BLOCK, dtype=np.int32)[None, :]
    dist = qi + BLOCK - kj
    n = np.maximum(dist, 0)
    max_exact = N_BUCKETS // 2
    nf = np.maximum(n, 1).astype(np.float32)
    large = max_exact + (np.log(nf / np.float32(max_exact)) / np.float32(np.log(MAX_DISTANCE / max_exact))
                         * np.float32(N_BUCKETS - max_exact)).astype(np.int32)
    large = np.minimum(large, N_BUCKETS - 1)
    bucket = np.where(n < max_exact, n, large)
    valid = (dist >= 0) & (dist < BLOCK)
    return np.where(valid, bucket, -1).astype(np.int32)


def _head_mean_matrix():
    idx = np.arange(2 * LANES) // HEAD_DIM
    return (idx[:, None] == idx[None, :]).astype(np.float32) / HEAD_DIM


def _dot(a, b):
    return jnp.dot(a, b, preferred_element_type=F32)


def _dot_nt(a, b):
    return lax.dot_general(a, b, (((1,), (1,)), ((), ())), preferred_element_type=F32)


def _head_rms(t, hm, gain, scale):
    sq = (t * t).astype(BF16)
    parts = []
    for c in range(t.shape[1] // (2 * LANES)):
        parts.append(_dot(sq[:, c * 2 * LANES:(c + 1) * 2 * LANES], hm))
    ms = parts[0] if len(parts) == 1 else jnp.concatenate(parts, axis=1)
    return t * lax.rsqrt(ms + EPS) * gain * scale


def _mixer_kernel(x_ref, bucket_ref, relb_ref, sink_ref, hm_ref,
                  g_mix_ref, gq_ref, gk_ref, w_q_ref, w_k_ref, w_v_ref, w_a_ref, w_g_ref,
                  w_ga_ref, w_gc_ref, w_ao_ref, w_dw_ref, b_dw_ref, ln_g_ref, ln_b_ref,
                  w_co_ref, w_out_ref,
                  o_ref,
                  bias_s, u_s, q_s, kdup_s, vdup_s, attn_s, h_s, conv_s, ga_s, gc_s):
    tile = x_ref.shape[1]
    n_blocks = tile // BLOCK
    b = pl.program_id(0)
    j = pl.program_id(1)

    @pl.when((b == 0) & (j == 0))
    def _build_bias():
        bk = bucket_ref[...]
        left = lax.broadcasted_iota(jnp.int32, bk.shape, 1) < BLOCK
        for h in range(N_Q_HEADS):
            acc = jnp.full(bk.shape, NEG, F32)
            for bb in range(N_BUCKETS):
                acc = jnp.where(bk == bb, relb_ref[bb, h] * LOG2E, acc)
            bias_s[0, h] = acc
            bias_s[1, h] = jnp.where(left, NEG, acc)

    @pl.when(j == 0)
    def _zero_halo():
        kdup_s[:, 0:BLOCK, :] = jnp.zeros((N_KV_HEADS, BLOCK, LANES), BF16)
        vdup_s[:, 0:BLOCK, :] = jnp.zeros((N_KV_HEADS, BLOCK, LANES), BF16)
        h_s[:, 0:CONV_HALO, :] = jnp.zeros((N_LANE_BLOCKS, CONV_HALO, LANES), F32)

    @pl.when(j != 0)
    def _carry_halo():
        kdup_s[:, 0:BLOCK, :] = kdup_s[:, tile:tile + BLOCK, :]
        vdup_s[:, 0:BLOCK, :] = vdup_s[:, tile:tile + BLOCK, :]
        h_s[:, 0:CONV_HALO, :] = h_s[:, tile:tile + CONV_HALO, :]

    x = x_ref[0]
    u = (x * lax.rsqrt(jnp.mean(x * x, axis=-1, keepdims=True) + EPS) * g_mix_ref[...]).astype(BF16)
    hm = hm_ref[...]

    hglu = _dot(u, w_a_ref[...]) * jax.nn.sigmoid(_dot(u, w_g_ref[...]))
    for c in range(N_LANE_BLOCKS):
        h_s[c, CONV_HALO:CONV_HALO + tile, :] = hglu[:, c * LANES:(c + 1) * LANES]

    kn = _head_rms(_dot(u, w_k_ref[...]), hm, gk_ref[...], 1.0)
    vv = _dot(u, w_v_ref[...])
    low = lax.broadcasted_iota(jnp.int32, (tile, LANES), 1) < HEAD_DIM
    for src, dst in ((kn, kdup_s), (vv, vdup_s)):
        for c in range(KV_WIDTH // LANES):
            s0 = src[:, c * LANES:(c + 1) * LANES]
            s1 = pltpu.roll(s0, HEAD_DIM, 1)
            dst[2 * c + 0, BLOCK:BLOCK + tile, :] = jnp.where(low, s0, s1).astype(BF16)
            dst[2 * c + 1, BLOCK:BLOCK + tile, :] = jnp.where(low, s1, s0).astype(BF16)
    u_s[...] = u

    def _conv_lane_block(c):
        for r in range(tile // CONV_ROWS):
            acc = jnp.broadcast_to(b_dw_ref[c], (CONV_ROWS, LANES))
            for t in range(CONV_WIDTH):
                win = h_s[c, r * CONV_ROWS + CONV_SKIP + t:r * CONV_ROWS + CONV_SKIP + t + CONV_ROWS, :]
                acc = acc + win * w_dw_ref[c, t:t + 1, :]
            conv_s[c, r * CONV_ROWS:(r + 1) * CONV_ROWS, :] = acc

    def _conv_and_q(i, carry):
        _conv_lane_block(i)
        t = _dot(u_s[...], w_q_ref[i])
        qn = _head_rms(t, hm, gq_ref[i], HEAD_DIM ** -0.5 * LOG2E)
        even = (lax.broadcasted_iota(jnp.int32, qn.shape, 1) & HEAD_DIM) == 0
        q_s[0, i] = jnp.where(even, qn, 0.0).astype(BF16)
        q_s[1, i] = jnp.where(even, 0.0, qn).astype(BF16)
        return carry

    def _conv_and_gates(i, carry):
        _conv_lane_block(i + N_COL_CHUNKS)
        ga_s[i] = jax.nn.sigmoid(_dot(u_s[...], w_ga_ref[i])).astype(BF16)
        gc_s[i] = jax.nn.sigmoid(_dot(u_s[...], w_gc_ref[i])).astype(BF16)
        return carry

    lax.fori_loop(0, N_COL_CHUNKS, _conv_and_q, 0)
    lax.fori_loop(0, N_COL_CHUNKS, _conv_and_gates, 0)

    def _attn_block(n, carry):
        r0 = pl.multiple_of(n * BLOCK, BLOCK)
        first = jnp.where((j == 0) & (n == 0), 1, 0)
        low = lax.broadcasted_iota(jnp.int32, (BLOCK, LANES), 1) < HEAD_DIM
        for g in range(N_KV_HEADS):
            q4 = jnp.concatenate(
                [q_s[i, g, pl.ds(r0, BLOCK), pr * LANES:(pr + 1) * LANES] for pr in range(2) for i in range(2)],
                axis=0)
            kb = kdup_s[g, pl.ds(r0, 2 * BLOCK), :]
            vb = vdup_s[g, pl.ds(r0, 2 * BLOCK), :]
            s4 = _dot_nt(q4, kb)
            es, rs = [], []
            for hh in range(GROUP):
                h = GROUP * g + hh
                s = s4[hh * BLOCK:(hh + 1) * BLOCK] + bias_s[first, h]
                m = jnp.max(s, axis=-1, keepdims=True)
                e = jnp.exp2(s - m)
                denom = jnp.sum(e, axis=-1, keepdims=True) + jnp.exp2(sink_ref[h] * LOG2E - m)
                es.append(e.astype(BF16))
                rs.append(1.0 / denom)
            o4 = _dot(jnp.concatenate(es, axis=0), vb)
            for pr in range(2):
                o_even = o4[(2 * pr) * BLOCK:(2 * pr + 1) * BLOCK] * rs[2 * pr]
                o_odd = o4[(2 * pr + 1) * BLOCK:(2 * pr + 2) * BLOCK] * rs[2 * pr + 1]
                p = 2 * g + pr
                attn_s[pl.ds(r0, BLOCK), p * LANES:(p + 1) * LANES] = jnp.where(low, o_even, o_odd).astype(BF16)
        return carry

    lax.fori_loop(0, n_blocks, _attn_block, 0)

    cv = jnp.concatenate([conv_s[c] for c in range(N_LANE_BLOCKS)], axis=1)
    mu = jnp.mean(cv, axis=-1, keepdims=True)
    xc = cv - mu
    var = jnp.mean(xc * xc, axis=-1, keepdims=True)
    y = xc * lax.rsqrt(var + EPS) * ln_g_ref[...] + ln_b_ref[...]
    conv = _dot((y * jax.nn.sigmoid(y)).astype(BF16), w_co_ref[...])

    attn = _dot(attn_s[...], w_ao_ref[...])
    gate_a = jnp.concatenate([ga_s[i] for i in range(N_COL_CHUNKS)], axis=1).astype(F32)
    gate_c = jnp.concatenate([gc_s[i] for i in range(N_COL_CHUNKS)], axis=1).astype(F32)
    merged = gate_a * attn + gate_c * conv
    o_ref[0] = x + _dot(merged.astype(BF16), w_out_ref[...])


def _mlp_kernel(x_ref, g_ref, w1_ref, w2_ref, o_ref):
    x = x_ref[...]
    xn = (x * lax.rsqrt(jnp.mean(x * x, axis=-1, keepdims=True) + EPS) * g_ref[...]).astype(BF16)
    acc = x
    for c in range(D_FF // FF_CHUNK):
        hmid = jnp.square(jnp.maximum(_dot(xn, w1_ref[:, c * FF_CHUNK:(c + 1) * FF_CHUNK]), 0.0))
        acc = acc + _dot(hmid.astype(BF16), w2_ref[c * FF_CHUNK:(c + 1) * FF_CHUNK, :])
    o_ref[...] = acc


def _resident(shape):
    zeros = (0,) * len(shape)
    return pl.BlockSpec(shape, lambda *_: zeros, pipeline_mode=pl.Buffered(1))


def _smem():
    return pl.BlockSpec(memory_space=pltpu.SMEM)


def kernel(x, norm_mix_g, w_in, q_norm_g, k_norm_g, attn_sinks, rel_bias, w_attn_o, w_dw, b_dw,
           conv_ln_g, conv_ln_b, w_conv_out, w_out, norm_mlp_g, w_ff1, w_ff2):
    batch, seq, d = x.shape
    assert d == D_MODEL and seq % MIX_TILE == 0 and (batch * seq) % MLP_TILE == 0
    layer = 0
    row = lambda v: v.reshape(1, -1).astype(F32)
    w = w_in[layer].astype(BF16)
    q_end = N_Q_HEADS * HEAD_DIM
    k_end = q_end + KV_WIDTH
    v_end = k_end + KV_WIDTH
    a_end = v_end + D_MODEL
    g_end = a_end + D_MODEL
    ga_end = g_end + D_MODEL
    lane_blocks = lambda v: v.astype(F32).reshape(-1, N_LANE_BLOCKS, LANES).transpose(1, 0, 2)
    col_chunks = lambda m: m.reshape(m.shape[0], N_COL_CHUNKS, COL_CHUNK).transpose(1, 0, 2)

    operands = [
        x,
        jnp.asarray(_bucket_table()),
        rel_bias.astype(F32),
        attn_sinks[layer].astype(F32),
        jnp.asarray(_head_mean_matrix(), dtype=BF16),
        row(norm_mix_g[layer]),
        jnp.tile(q_norm_g[layer], N_Q_HEADS).astype(F32).reshape(N_COL_CHUNKS, 1, COL_CHUNK),
        row(jnp.tile(k_norm_g[layer], N_KV_HEADS)),
        col_chunks(w[:, :q_end]), w[:, q_end:k_end], w[:, k_end:v_end], w[:, v_end:a_end], w[:, a_end:g_end],
        col_chunks(w[:, g_end:ga_end]), col_chunks(w[:, ga_end:]),
        w_attn_o[layer].astype(BF16),
        lane_blocks(w_dw[layer]),
        lane_blocks(b_dw[layer].reshape(1, -1)),
        row(conv_ln_g[layer]), row(conv_ln_b[layer]),
        w_conv_out[layer].astype(BF16),
        w_out[layer].astype(BF16),
    ]
    in_specs = [pl.BlockSpec((1, MIX_TILE, D_MODEL), lambda b, j: (b, j, 0)),
                _resident(operands[1].shape), _smem(), _smem()]
    in_specs += [_resident(op.shape) for op in operands[4:]]

    x1 = pl.pallas_call(
        _mixer_kernel,
        grid=(batch, seq // MIX_TILE),
        in_specs=in_specs,
        out_specs=pl.BlockSpec((1, MIX_TILE, D_MODEL), lambda b, j: (b, j, 0)),
        out_shape=jax.ShapeDtypeStruct(x.shape, F32),
        scratch_shapes=[
            pltpu.VMEM((2, N_Q_HEADS, BLOCK, 2 * BLOCK), F32),
            pltpu.VMEM((MIX_TILE, D_MODEL), BF16),
            pltpu.VMEM((2, N_COL_CHUNKS, MIX_TILE, COL_CHUNK), BF16),
            pltpu.VMEM((N_KV_HEADS, BLOCK + MIX_TILE, LANES), BF16),
            pltpu.VMEM((N_KV_HEADS, BLOCK + MIX_TILE, LANES), BF16),
            pltpu.VMEM((MIX_TILE, D_MODEL), BF16),
            pltpu.VMEM((N_LANE_BLOCKS, CONV_HALO + MIX_TILE, LANES), F32),
            pltpu.VMEM((N_LANE_BLOCKS, MIX_TILE, LANES), F32),
            pltpu.VMEM((N_COL_CHUNKS, MIX_TILE, COL_CHUNK), BF16),
            pltpu.VMEM((N_COL_CHUNKS, MIX_TILE, COL_CHUNK), BF16),
        ],
        compiler_params=pltpu.CompilerParams(
            dimension_semantics=("arbitrary", "arbitrary"), vmem_limit_bytes=VMEM_LIMIT),
        name="mixer",
    )(*operands)

    tokens = batch * seq
    out = pl.pallas_call(
        _mlp_kernel,
        grid=(tokens // MLP_TILE,),
        in_specs=[pl.BlockSpec((MLP_TILE, D_MODEL), lambda i: (i, 0)),
                  _resident((1, D_MODEL)), _resident((D_MODEL, D_FF)), _resident((D_FF, D_MODEL))],
        out_specs=pl.BlockSpec((MLP_TILE, D_MODEL), lambda i: (i, 0)),
        out_shape=jax.ShapeDtypeStruct((tokens, D_MODEL), F32),
        compiler_params=pltpu.CompilerParams(
            dimension_semantics=("arbitrary",), vmem_limit_bytes=VMEM_LIMIT),
        name="mlp",
    )(x1.reshape(tokens, D_MODEL), row(norm_mlp_g[layer]),
      w_ff1[layer].astype(BF16), w_ff2[layer].astype(BF16))
    return out.reshape(batch, seq, D_MODEL)
```

```python
import functools

import numpy as np
import jax
import jax.numpy as jnp
from jax import lax
from jax.experimental import pallas as pl
from jax.experimental.pallas import tpu as pltpu

D_MODEL = 1024
HEAD_DIM = 64
N_Q_HEADS = 16
N_KV_HEADS = 4
KV_WIDTH = N_KV_HEADS * HEAD_DIM
GROUP = N_Q_HEADS // N_KV_HEADS
BLOCK = 128
CONV_WIDTH = 31
D_FF = 4 * D_MODEL
N_BUCKETS = 32
MAX_DISTANCE = 128
EPS = 1e-6
NEG = -1e30
LOG2E = 1.4426950408889634

LANES = 128
N_LANE_BLOCKS = D_MODEL // LANES
COL_CHUNK = 2 * LANES
N_COL_CHUNKS = D_MODEL // COL_CHUNK
CONV_HALO = 32
CONV_SKIP = CONV_HALO - (CONV_WIDTH - 1)
CONV_ROWS = 64
FF_CHUNK = D_FF // N_LANE_BLOCKS

TILE = 512
VMEM_LIMIT = 58 * 1024 * 1024

F32 = jnp.float32
BF16 = jnp.bfloat16


def _bucket_table():
    qi = np.arange(BLOCK, dtype=np.int32)[:, None]
    kj = np.arange(2 * BLOCK, dtype=np.int32)[None, :]
    dist = qi + BLOCK - kj
    n = np.maximum(dist, 0)
    max_exact = N_BUCKETS // 2
    nf = np.maximum(n, 1).astype(np.float32)
    large = max_exact + (np.log(nf / np.float32(max_exact)) / np.float32(np.log(MAX_DISTANCE / max_exact))
                         * np.float32(N_BUCKETS - max_exact)).astype(np.int32)
    large = np.minimum(large, N_BUCKETS - 1)
    bucket = np.where(n < max_exact, n, large)
    valid = (dist >= 0) & (dist < BLOCK)
    return np.where(valid, bucket, -1).astype(np.int32)


def _head_mean_matrix():
    idx = np.arange(COL_CHUNK) // HEAD_DIM
    return (idx[:, None] == idx[None, :]).astype(np.float32) / HEAD_DIM


def _dot(a, b):
    return jnp.dot(a, b, preferred_element_type=F32)


def _dot_nt(a, b):
    return lax.dot_general(a, b, (((1,), (1,)), ((), ())), preferred_element_type=F32)


def _rms(x, gain):
    return x * lax.rsqrt(jnp.mean(x * x, axis=-1, keepdims=True) + EPS) * gain


def _head_rms(t, hm, gain, scale):
    ms = _dot((t * t).astype(BF16), hm)
    return t * lax.rsqrt(ms + EPS) * gain * scale


def _proj_attn_kernel(x_ref, bucket_ref, relb_ref, sink_ref, hm_ref, g_mix_ref, gq_ref, gk_ref,
                      w_q_ref, w_kv_ref, w_a_ref, w_g_ref, w_ga_ref, w_gc_ref,
                      attn_ref, h_ref, ga_ref, gc_ref,
                      bias_s, u_s, q_s, kdup_s, vdup_s):
    tile = x_ref.shape[1]
    b = pl.program_id(0)
    j = pl.program_id(1)

    @pl.when((b == 0) & (j == 0))
    def _build_bias():
        bk = bucket_ref[...]
        left = lax.broadcasted_iota(jnp.int32, bk.shape, 1) < BLOCK
        for h in range(N_Q_HEADS):
            acc = jnp.full(bk.shape, NEG, F32)
            for bb in range(N_BUCKETS):
                acc = jnp.where(bk == bb, relb_ref[bb, h] * LOG2E, acc)
            bias_s[0, h] = acc
            bias_s[1, h] = jnp.where(left, NEG, acc)

    @pl.when(j == 0)
    def _zero_halo():
        kdup_s[:, 0:BLOCK, :] = jnp.zeros((N_KV_HEADS, BLOCK, LANES), BF16)
        vdup_s[:, 0:BLOCK, :] = jnp.zeros((N_KV_HEADS, BLOCK, LANES), BF16)

    @pl.when(j != 0)
    def _carry_halo():
        kdup_s[:, 0:BLOCK, :] = kdup_s[:, tile:tile + BLOCK, :]
        vdup_s[:, 0:BLOCK, :] = vdup_s[:, tile:tile + BLOCK, :]

    u_s[...] = _rms(x_ref[0], g_mix_ref[...]).astype(BF16)
    hm = hm_ref[...]

    h_ref[0] = _dot(u_s[...], w_a_ref[...]) * jax.nn.sigmoid(_dot(u_s[...], w_g_ref[...]))
    ga_ref[0] = jax.nn.sigmoid(_dot(u_s[...], w_ga_ref[...])).astype(BF16)
    gc_ref[0] = jax.nn.sigmoid(_dot(u_s[...], w_gc_ref[...])).astype(BF16)

    q = _dot(u_s[...], w_q_ref[...])
    kv = _dot(u_s[...], w_kv_ref[...])
    for c in range(N_COL_CHUNKS):
        cols = slice(c * COL_CHUNK, (c + 1) * COL_CHUNK)
        qn = _head_rms(q[:, cols], hm, gq_ref[:, cols], HEAD_DIM ** -0.5 * LOG2E)
        even = (lax.broadcasted_iota(jnp.int32, qn.shape, 1) & HEAD_DIM) == 0
        q_s[0, c] = jnp.where(even, qn, 0.0).astype(BF16)
        q_s[1, c] = jnp.where(even, 0.0, qn).astype(BF16)

    kn = _head_rms(kv[:, :KV_WIDTH], hm, gk_ref[...], 1.0)
    vv = kv[:, KV_WIDTH:]
    low = lax.broadcasted_iota(jnp.int32, (tile, LANES), 1) < HEAD_DIM
    for src, dst in ((kn, kdup_s), (vv, vdup_s)):
        for c in range(KV_WIDTH // LANES):
            s0 = src[:, c * LANES:(c + 1) * LANES]
            s1 = pltpu.roll(s0, HEAD_DIM, 1)
            dst[2 * c + 0, BLOCK:BLOCK + tile, :] = jnp.where(low, s0, s1).astype(BF16)
            dst[2 * c + 1, BLOCK:BLOCK + tile, :] = jnp.where(low, s1, s0).astype(BF16)

    def _attn_block(n, carry):
        r0 = pl.multiple_of(n * BLOCK, BLOCK)
        first = jnp.where((j == 0) & (n == 0), 1, 0)
        low_b = lax.broadcasted_iota(jnp.int32, (BLOCK, LANES), 1) < HEAD_DIM
        for g in range(N_KV_HEADS):
            q4 = jnp.concatenate(
                [q_s[i, g, pl.ds(r0, BLOCK), pr * LANES:(pr + 1) * LANES] for pr in range(2) for i in range(2)],
                axis=0)
            kb = kdup_s[g, pl.ds(r0, 2 * BLOCK), :]
            vb = vdup_s[g, pl.ds(r0, 2 * BLOCK), :]
            s4 = _dot_nt(q4, kb)
            es, rs = [], []
            for hh in range(GROUP):
                h = GROUP * g + hh
                s = s4[hh * BLOCK:(hh + 1) * BLOCK] + bias_s[first, h]
                m = jnp.max(s, axis=-1, keepdims=True)
                e = jnp.exp2(s - m)
                denom = jnp.sum(e, axis=-1, keepdims=True) + jnp.exp2(sink_ref[h] * LOG2E - m)
                es.append(e.astype(BF16))
                rs.append(1.0 / denom)
            o4 = _dot(jnp.concatenate(es, axis=0), vb)
            for pr in range(2):
                o_even = o4[(2 * pr) * BLOCK:(2 * pr + 1) * BLOCK] * rs[2 * pr]
                o_odd = o4[(2 * pr + 1) * BLOCK:(2 * pr + 2) * BLOCK] * rs[2 * pr + 1]
                p = 2 * g + pr
                attn_ref[0, pl.ds(r0, BLOCK), p * LANES:(p + 1) * LANES] = (
                    jnp.where(low_b, o_even, o_odd).astype(BF16))
        return carry

    lax.fori_loop(0, tile // BLOCK, _attn_block, 0)


def _conv_mlp_kernel(x_ref, attn_ref, ga_ref, gc_ref, h_ref,
                     w_dw_ref, b_dw_ref, ln_g_ref, ln_b_ref, w_co_ref, w_ao_ref, w_out_ref,
                     g_mlp_ref, w1_ref, w2_ref,
                     o_ref,
                     hbuf_s, conv_s, xn_s, *, tiles_per_seq):
    tile = x_ref.shape[1]
    i = pl.program_id(0)

    @pl.when(i % tiles_per_seq == 0)
    def _zero_halo():
        hbuf_s[:, 0:CONV_HALO, :] = jnp.zeros((N_LANE_BLOCKS, CONV_HALO, LANES), F32)

    @pl.when(i % tiles_per_seq != 0)
    def _carry_halo():
        hbuf_s[:, 0:CONV_HALO, :] = hbuf_s[:, tile:tile + CONV_HALO, :]

    for c in range(N_LANE_BLOCKS):
        hbuf_s[c, CONV_HALO:CONV_HALO + tile, :] = h_ref[0, :, c * LANES:(c + 1) * LANES]

    def _conv_lane_block(c):
        for r in range(tile // CONV_ROWS):
            acc = jnp.broadcast_to(b_dw_ref[c], (CONV_ROWS, LANES))
            for t in range(CONV_WIDTH):
                win = hbuf_s[c, r * CONV_ROWS + CONV_SKIP + t:r * CONV_ROWS + CONV_SKIP + t + CONV_ROWS, :]
                acc = acc + win * w_dw_ref[c, t:t + 1, :]
            conv_s[c, r * CONV_ROWS:(r + 1) * CONV_ROWS, :] = acc

    @pl.when(i == 0)
    def _first_conv():
        def _step(c, carry):
            _conv_lane_block(c)
            return carry
        lax.fori_loop(0, N_LANE_BLOCKS, _step, 0)

    @pl.when(i > 0)
    def _dense_and_conv():
        cv = jnp.concatenate([conv_s[c] for c in range(N_LANE_BLOCKS)], axis=1)
        mu = jnp.mean(cv, axis=-1, keepdims=True)
        xc = cv - mu
        var = jnp.mean(xc * xc, axis=-1, keepdims=True)
        y = xc * lax.rsqrt(var + EPS) * ln_g_ref[...] + ln_b_ref[...]
        conv = _dot((y * jax.nn.sigmoid(y)).astype(BF16), w_co_ref[...])
        attn = _dot(attn_ref[0], w_ao_ref[...])
        merged = ga_ref[0].astype(F32) * attn + gc_ref[0].astype(F32) * conv
        x1 = x_ref[0] + _dot(merged.astype(BF16), w_out_ref[...])
        xn_s[...] = _rms(x1, g_mlp_ref[...]).astype(BF16)
        o_ref[0] = x1

        def _step(c, carry):
            _conv_lane_block(c)
            hmid = jnp.square(jnp.maximum(_dot(xn_s[...], w1_ref[c]), 0.0))
            o_ref[0] += _dot(hmid.astype(BF16), w2_ref[c])
            return carry
        lax.fori_loop(0, N_LANE_BLOCKS, _step, 0)


def _resident(shape):
    zeros = (0,) * len(shape)
    return pl.BlockSpec(shape, lambda *_: zeros, pipeline_mode=pl.Buffered(1))


def _smem():
    return pl.BlockSpec(memory_space=pltpu.SMEM)


def kernel(x, norm_mix_g, w_in, q_norm_g, k_norm_g, attn_sinks, rel_bias, w_attn_o, w_dw, b_dw,
           conv_ln_g, conv_ln_b, w_conv_out, w_out, norm_mlp_g, w_ff1, w_ff2):
    batch, seq, d = x.shape
    assert d == D_MODEL and seq % TILE == 0
    tiles_per_seq = seq // TILE
    n_tiles = batch * tiles_per_seq
    layer = 0
    row = lambda v: v.reshape(1, -1).astype(F32)
    w = w_in[layer].astype(BF16)
    q_end = N_Q_HEADS * HEAD_DIM
    k_end = q_end + KV_WIDTH
    v_end = k_end + KV_WIDTH
    a_end = v_end + D_MODEL
    g_end = a_end + D_MODEL
    ga_end = g_end + D_MODEL
    lane_blocks = lambda v: v.astype(F32).reshape(-1, N_LANE_BLOCKS, LANES).transpose(1, 0, 2)

    tile_spec = pl.BlockSpec((1, TILE, D_MODEL), lambda b, j: (b, j, 0))
    operands = [
        x,
        jnp.asarray(_bucket_table()),
        rel_bias.astype(F32),
        attn_sinks[layer].astype(F32),
        jnp.asarray(_head_mean_matrix(), dtype=BF16),
        row(norm_mix_g[layer]),
        row(jnp.tile(q_norm_g[layer], N_Q_HEADS)),
        row(jnp.tile(k_norm_g[layer], N_KV_HEADS)),
        w[:, :q_end], w[:, q_end:v_end], w[:, v_end:a_end], w[:, a_end:g_end],
        w[:, g_end:ga_end], w[:, ga_end:],
    ]
    in_specs = [tile_spec, _resident(operands[1].shape), _smem(), _smem()]
    in_specs += [_resident(op.shape) for op in operands[4:]]
    tile_shape = lambda dtype: jax.ShapeDtypeStruct(x.shape, dtype)
    attn, h, gate_a, gate_c = pl.pallas_call(
        _proj_attn_kernel,
        grid=(batch, tiles_per_seq),
        in_specs=in_specs,
        out_specs=[tile_spec] * 4,
        out_shape=[tile_shape(BF16), tile_shape(F32), tile_shape(BF16), tile_shape(BF16)],
        scratch_shapes=[
            pltpu.VMEM((2, N_Q_HEADS, BLOCK, 2 * BLOCK), F32),
            pltpu.VMEM((TILE, D_MODEL), BF16),
            pltpu.VMEM((2, N_COL_CHUNKS, TILE, COL_CHUNK), BF16),
            pltpu.VMEM((N_KV_HEADS, BLOCK + TILE, LANES), BF16),
            pltpu.VMEM((N_KV_HEADS, BLOCK + TILE, LANES), BF16),
        ],
        compiler_params=pltpu.CompilerParams(
            dimension_semantics=("arbitrary", "arbitrary"), vmem_limit_bytes=VMEM_LIMIT),
        name="proj_attn",
    )(*operands)

    done_tile = lambda i: (jnp.maximum(i - 1, 0) // tiles_per_seq, jnp.maximum(i - 1, 0) % tiles_per_seq, 0)
    conv_tile = lambda i: (jnp.minimum(i, n_tiles - 1) // tiles_per_seq,
                           jnp.minimum(i, n_tiles - 1) % tiles_per_seq, 0)
    done_spec = pl.BlockSpec((1, TILE, D_MODEL), done_tile)
    ff_chunks_in = lambda m: m.reshape(D_MODEL, N_LANE_BLOCKS, FF_CHUNK).transpose(1, 0, 2)
    operands2 = [
        x, attn, gate_a, gate_c, h,
        lane_blocks(w_dw[layer]),
        lane_blocks(b_dw[layer].reshape(1, -1)),
        row(conv_ln_g[layer]), row(conv_ln_b[layer]),
        w_conv_out[layer].astype(BF16), w_attn_o[layer].astype(BF16), w_out[layer].astype(BF16),
        row(norm_mlp_g[layer]),
        ff_chunks_in(w_ff1[layer].astype(BF16)),
        w_ff2[layer].astype(BF16).reshape(N_LANE_BLOCKS, FF_CHUNK, D_MODEL),
    ]
    in_specs2 = [done_spec] * 4 + [pl.BlockSpec((1, TILE, D_MODEL), conv_tile)]
    in_specs2 += [_resident(op.shape) for op in operands2[5:]]
    out = pl.pallas_call(
        functools.partial(_conv_mlp_kernel, tiles_per_seq=tiles_per_seq),
        grid=(n_tiles + 1,),
        in_specs=in_specs2,
        out_specs=done_spec,
        out_shape=jax.ShapeDtypeStruct(x.shape, F32),
        scratch_shapes=[
            pltpu.VMEM((N_LANE_BLOCKS, CONV_HALO + TILE, LANES), F32),
            pltpu.VMEM((N_LANE_BLOCKS, TILE, LANES), F32),
            pltpu.VMEM((TILE, D_MODEL), BF16),
        ],
        compiler_params=pltpu.CompilerParams(
            dimension_semantics=("arbitrary",), vmem_limit_bytes=VMEM_LIMIT),
        name="conv_mlp",
    )(*operands2)
    return out
```

```python
import functools

import numpy as np
import jax
import jax.numpy as jnp
from jax import lax
from jax.experimental import pallas as pl
from jax.experimental.pallas import tpu as pltpu

D_MODEL = 1024
HEAD_DIM = 64
N_Q_HEADS = 16
N_KV_HEADS = 4
KV_WIDTH = N_KV_HEADS * HEAD_DIM
GROUP = N_Q_HEADS // N_KV_HEADS
BLOCK = 128
CONV_WIDTH = 31
D_FF = 4 * D_MODEL
N_BUCKETS = 32
MAX_DISTANCE = 128
EPS = 1e-6
NEG = -1e30
LOG2E = 1.4426950408889634

LANES = 128
SUBLANES = 8
N_LANE_BLOCKS = D_MODEL // LANES
COL_CHUNK = 2 * LANES
N_COL_CHUNKS = D_MODEL // COL_CHUNK
CONV_HALO = 32
CONV_SKIP = CONV_HALO - (CONV_WIDTH - 1)
CONV_ROWS = 64
FF_CHUNK = D_FF // N_LANE_BLOCKS

TILE = 512
VMEM_LIMIT = 58 * 1024 * 1024

F32 = jnp.float32
BF16 = jnp.bfloat16


def _bucket_table():
    qi = np.arange(BLOCK, dtype=np.int32)[:, None]
    kj = np.arange(2 * BLOCK, dtype=np.int32)[None, :]
    dist = qi + BLOCK - kj
    n = np.maximum(dist, 0)
    max_exact = N_BUCKETS // 2
    nf = np.maximum(n, 1).astype(np.float32)
    large = max_exact + (np.log(nf / np.float32(max_exact)) / np.float32(np.log(MAX_DISTANCE / max_exact))
                         * np.float32(N_BUCKETS - max_exact)).astype(np.int32)
    large = np.minimum(large, N_BUCKETS - 1)
    bucket = np.where(n < max_exact, n, large)
    valid = (dist >= 0) & (dist < BLOCK)
    return np.where(valid, bucket, -1).astype(np.int32)


def _head_mean_matrix():
    idx = np.arange(COL_CHUNK) // HEAD_DIM
    return (idx[:, None] == idx[None, :]).astype(np.float32) / HEAD_DIM


def _dot(a, b):
    return jnp.dot(a, b, preferred_element_type=F32)


def _dot_nt(a, b):
    return lax.dot_general(a, b, (((1,), (1,)), ((), ())), preferred_element_type=F32)


def _rms(x, gain):
    return x * lax.rsqrt(jnp.mean(x * x, axis=-1, keepdims=True) + EPS) * gain


def _head_rms(t, hm, gain, scale):
    ms = _dot((t * t).astype(BF16), hm)
    return t * lax.rsqrt(ms + EPS) * gain * scale


def _proj_attn_kernel(x_ref, bucket_ref, relb_ref, sink_ref, hm_ref, g_mix_ref, gq_ref, gk_ref,
                      w_q_ref, w_kv_ref, w_a_ref, w_g_ref, w_ga_ref, w_gc_ref,
                      attn_ref, h_ref, ga_ref, gc_ref,
                      bias_s, u_s, q_s, kdup_s, vdup_s):
    tile = x_ref.shape[1]
    b = pl.program_id(0)
    j = pl.program_id(1)

    @pl.when((b == 0) & (j == 0))
    def _build_bias():
        bk = bucket_ref[...]
        left = lax.broadcasted_iota(jnp.int32, bk.shape, 1) < BLOCK
        for h in range(N_Q_HEADS):
            acc = jnp.full(bk.shape, NEG, F32)
            for bb in range(N_BUCKETS):
                acc = jnp.where(bk == bb, relb_ref[bb, h] * LOG2E, acc)
            bias_s[0, h] = acc
            bias_s[1, h] = jnp.where(left, NEG, acc)

    @pl.when(j == 0)
    def _zero_halo():
        kdup_s[:, 0:BLOCK, :] = jnp.zeros((N_KV_HEADS, BLOCK, LANES), BF16)
        vdup_s[:, 0:BLOCK, :] = jnp.zeros((N_KV_HEADS, BLOCK, LANES), BF16)

    @pl.when(j != 0)
    def _carry_halo():
        kdup_s[:, 0:BLOCK, :] = kdup_s[:, tile:tile + BLOCK, :]
        vdup_s[:, 0:BLOCK, :] = vdup_s[:, tile:tile + BLOCK, :]

    u_s[...] = _rms(x_ref[0], g_mix_ref[...]).astype(BF16)
    hm = hm_ref[...]

    h_ref[0] = _dot(u_s[...], w_a_ref[...]) * jax.nn.sigmoid(_dot(u_s[...], w_g_ref[...]))
    ga_ref[0] = jax.nn.sigmoid(_dot(u_s[...], w_ga_ref[...])).astype(BF16)
    gc_ref[0] = jax.nn.sigmoid(_dot(u_s[...], w_gc_ref[...])).astype(BF16)

    q = _dot(u_s[...], w_q_ref[...])
    kv = _dot(u_s[...], w_kv_ref[...])
    for c in range(N_COL_CHUNKS):
        cols = slice(c * COL_CHUNK, (c + 1) * COL_CHUNK)
        qn = _head_rms(q[:, cols], hm, gq_ref[:, cols], HEAD_DIM ** -0.5 * LOG2E)
        even = (lax.broadcasted_iota(jnp.int32, qn.shape, 1) & HEAD_DIM) == 0
        q_s[0, c] = jnp.where(even, qn, 0.0).astype(BF16)
        q_s[1, c] = jnp.where(even, 0.0, qn).astype(BF16)

    kn = _head_rms(kv[:, :KV_WIDTH], hm, gk_ref[...], 1.0)
    vv = kv[:, KV_WIDTH:]
    low = lax.broadcasted_iota(jnp.int32, (tile, LANES), 1) < HEAD_DIM
    for src, dst in ((kn, kdup_s), (vv, vdup_s)):
        for c in range(KV_WIDTH // LANES):
            s0 = src[:, c * LANES:(c + 1) * LANES]
            s1 = pltpu.roll(s0, HEAD_DIM, 1)
            dst[2 * c + 0, BLOCK:BLOCK + tile, :] = jnp.where(low, s0, s1).astype(BF16)
            dst[2 * c + 1, BLOCK:BLOCK + tile, :] = jnp.where(low, s1, s0).astype(BF16)

    def _attn_block(n, carry):
        r0 = pl.multiple_of(n * BLOCK, BLOCK)
        first = jnp.where((j == 0) & (n == 0), 1, 0)
        low_b = lax.broadcasted_iota(jnp.int32, (BLOCK, LANES), 1) < HEAD_DIM
        for g in range(N_KV_HEADS):
            q4 = jnp.concatenate(
                [q_s[i, g, pl.ds(r0, BLOCK), pr * LANES:(pr + 1) * LANES] for pr in range(2) for i in range(2)],
                axis=0)
            kb = kdup_s[g, pl.ds(r0, 2 * BLOCK), :]
            vb = vdup_s[g, pl.ds(r0, 2 * BLOCK), :]
            s4 = _dot_nt(q4, kb)
            es, rs = [], []
            for hh in range(GROUP):
                h = GROUP * g + hh
                s = s4[hh * BLOCK:(hh + 1) * BLOCK] + bias_s[first, h]
                m = jnp.max(s, axis=-1, keepdims=True)
                e = jnp.exp2(s - m)
                denom = jnp.sum(e, axis=-1, keepdims=True) + jnp.exp2(sink_ref[h] * LOG2E - m)
                es.append(e.astype(BF16))
                rs.append(1.0 / denom)
            o4 = _dot(jnp.concatenate(es, axis=0), vb)
            for pr in range(2):
                o_even = o4[(2 * pr) * BLOCK:(2 * pr + 1) * BLOCK] * rs[2 * pr]
                o_odd = o4[(2 * pr + 1) * BLOCK:(2 * pr + 2) * BLOCK] * rs[2 * pr + 1]
                p = 2 * g + pr
                attn_ref[0, pl.ds(r0, BLOCK), p * LANES:(p + 1) * LANES] = (
                    jnp.where(low_b, o_even, o_odd).astype(BF16))
        return carry

    lax.fori_loop(0, tile // BLOCK, _attn_block, 0)


def _conv_mlp_kernel(x_ref, attn_ref, ga_ref, gc_ref, h_ref,
                     w_dw_ref, b_dw_ref, ln_g_ref, ln_b_ref, w_co_ref, w_ao_ref, w_out_ref,
                     g_mlp_ref, w1_ref, w2_ref,
                     o_ref,
                     hbuf_s, conv_s, xn_s, *, tiles_per_seq):
    tile = x_ref.shape[1]
    i = pl.program_id(0)

    @pl.when(i % tiles_per_seq == 0)
    def _zero_halo():
        hbuf_s[:, 0:CONV_HALO, :] = jnp.zeros((N_LANE_BLOCKS, CONV_HALO, LANES), F32)

    @pl.when(i % tiles_per_seq != 0)
    def _carry_halo():
        hbuf_s[:, 0:CONV_HALO, :] = hbuf_s[:, tile:tile + CONV_HALO, :]

    for c in range(N_LANE_BLOCKS):
        hbuf_s[c, CONV_HALO:CONV_HALO + tile, :] = h_ref[0, :, c * LANES:(c + 1) * LANES]

    def _conv_lane_block(c):
        for r in range(tile // CONV_ROWS):
            acc = jnp.broadcast_to(b_dw_ref[c], (CONV_ROWS, LANES))
            for a in range(SUBLANES):
                taps = range(a, CONV_WIDTH, SUBLANES)
                start = r * CONV_ROWS + CONV_SKIP + a
                rows = hbuf_s[c, start:start + CONV_ROWS + SUBLANES * (len(taps) - 1), :]
                for k, t in enumerate(taps):
                    acc = acc + rows[SUBLANES * k:SUBLANES * k + CONV_ROWS] * w_dw_ref[c, t:t + 1, :]
            conv_s[c, r * CONV_ROWS:(r + 1) * CONV_ROWS, :] = acc

    @pl.when(i == 0)
    def _first_conv():
        def _step(c, carry):
            _conv_lane_block(c)
            return carry
        lax.fori_loop(0, N_LANE_BLOCKS, _step, 0)

    @pl.when(i > 0)
    def _dense_and_conv():
        cv = jnp.concatenate([conv_s[c] for c in range(N_LANE_BLOCKS)], axis=1)
        mu = jnp.mean(cv, axis=-1, keepdims=True)
        xc = cv - mu
        var = jnp.mean(xc * xc, axis=-1, keepdims=True)
        y = xc * lax.rsqrt(var + EPS) * ln_g_ref[...] + ln_b_ref[...]
        conv = _dot((y * jax.nn.sigmoid(y)).astype(BF16), w_co_ref[...])
        attn = _dot(attn_ref[0], w_ao_ref[...])
        merged = ga_ref[0].astype(F32) * attn + gc_ref[0].astype(F32) * conv
        x1 = x_ref[0] + _dot(merged.astype(BF16), w_out_ref[...])
        xn_s[...] = _rms(x1, g_mlp_ref[...]).astype(BF16)
        o_ref[0] = x1

        def _step(c, carry):
            _conv_lane_block(c)
            hmid = jnp.square(jnp.maximum(_dot(xn_s[...], w1_ref[c]), 0.0))
            o_ref[0] += _dot(hmid.astype(BF16), w2_ref[c])
            return carry
        lax.fori_loop(0, N_LANE_BLOCKS, _step, 0)


def _resident(shape):
    zeros = (0,) * len(shape)
    return pl.BlockSpec(shape, lambda *_: zeros, pipeline_mode=pl.Buffered(1))


def _smem():
    return pl.BlockSpec(memory_space=pltpu.SMEM)


def kernel(x, norm_mix_g, w_in, q_norm_g, k_norm_g, attn_sinks, rel_bias, w_attn_o, w_dw, b_dw,
           conv_ln_g, conv_ln_b, w_conv_out, w_out, norm_mlp_g, w_ff1, w_ff2):
    batch, seq, d = x.shape
    assert d == D_MODEL and seq % TILE == 0
    tiles_per_seq = seq // TILE
    n_tiles = batch * tiles_per_seq
    layer = 0
    row = lambda v: v.reshape(1, -1).astype(F32)
    w = w_in[layer].astype(BF16)
    q_end = N_Q_HEADS * HEAD_DIM
    k_end = q_end + KV_WIDTH
    v_end = k_end + KV_WIDTH
    a_end = v_end + D_MODEL
    g_end = a_end + D_MODEL
    ga_end = g_end + D_MODEL
    lane_blocks = lambda v: v.astype(F32).reshape(-1, N_LANE_BLOCKS, LANES).transpose(1, 0, 2)

    tile_spec = pl.BlockSpec((1, TILE, D_MODEL), lambda b, j: (b, j, 0))
    operands = [
        x,
        jnp.asarray(_bucket_table()),
        rel_bias.astype(F32),
        attn_sinks[layer].astype(F32),
        jnp.asarray(_head_mean_matrix(), dtype=BF16),
        row(norm_mix_g[layer]),
        row(jnp.tile(q_norm_g[layer], N_Q_HEADS)),
        row(jnp.tile(k_norm_g[layer], N_KV_HEADS)),
        w[:, :q_end], w[:, q_end:v_end], w[:, v_end:a_end], w[:, a_end:g_end],
        w[:, g_end:ga_end], w[:, ga_end:],
    ]
    in_specs = [tile_spec, _resident(operands[1].shape), _smem(), _smem()]
    in_specs += [_resident(op.shape) for op in operands[4:]]
    tile_shape = lambda dtype: jax.ShapeDtypeStruct(x.shape, dtype)
    attn, h, gate_a, gate_c = pl.pallas_call(
        _proj_attn_kernel,
        grid=(batch, tiles_per_seq),
        in_specs=in_specs,
        out_specs=[tile_spec] * 4,
        out_shape=[tile_shape(BF16), tile_shape(F32), tile_shape(BF16), tile_shape(BF16)],
        scratch_shapes=[
            pltpu.VMEM((2, N_Q_HEADS, BLOCK, 2 * BLOCK), F32),
            pltpu.VMEM((TILE, D_MODEL), BF16),
            pltpu.VMEM((2, N_COL_CHUNKS, TILE, COL_CHUNK), BF16),
            pltpu.VMEM((N_KV_HEADS, BLOCK + TILE, LANES), BF16),
            pltpu.VMEM((N_KV_HEADS, BLOCK + TILE, LANES), BF16),
        ],
        compiler_params=pltpu.CompilerParams(
            dimension_semantics=("arbitrary", "arbitrary"), vmem_limit_bytes=VMEM_LIMIT),
        name="proj_attn",
    )(*operands)

    done_tile = lambda i: (jnp.maximum(i - 1, 0) // tiles_per_seq, jnp.maximum(i - 1, 0) % tiles_per_seq, 0)
    conv_tile = lambda i: (jnp.minimum(i, n_tiles - 1) // tiles_per_seq,
                           jnp.minimum(i, n_tiles - 1) % tiles_per_seq, 0)
    done_spec = pl.BlockSpec((1, TILE, D_MODEL), done_tile)
    ff_chunks_in = lambda m: m.reshape(D_MODEL, N_LANE_BLOCKS, FF_CHUNK).transpose(1, 0, 2)
    operands2 = [
        x, attn, gate_a, gate_c, h,
        lane_blocks(w_dw[layer]),
        lane_blocks(b_dw[layer].reshape(1, -1)),
        row(conv_ln_g[layer]), row(conv_ln_b[layer]),
        w_conv_out[layer].astype(BF16), w_attn_o[layer].astype(BF16), w_out[layer].astype(BF16),
        row(norm_mlp_g[layer]),
        ff_chunks_in(w_ff1[layer].astype(BF16)),
        w_ff2[layer].astype(BF16).reshape(N_LANE_BLOCKS, FF_CHUNK, D_MODEL),
    ]
    in_specs2 = [done_spec] * 4 + [pl.BlockSpec((1, TILE, D_MODEL), conv_tile)]
    in_specs2 += [_resident(op.shape) for op in operands2[5:]]
    out = pl.pallas_call(
        functools.partial(_conv_mlp_kernel, tiles_per_seq=tiles_per_seq),
        grid=(n_tiles + 1,),
        in_specs=in_specs2,
        out_specs=done_spec,
        out_shape=jax.ShapeDtypeStruct(x.shape, F32),
        scratch_shapes=[
            pltpu.VMEM((N_LANE_BLOCKS, CONV_HALO + TILE, LANES), F32),
            pltpu.VMEM((N_LANE_BLOCKS, TILE, LANES), F32),
            pltpu.VMEM((TILE, D_MODEL), BF16),
        ],
        compiler_params=pltpu.CompilerParams(
            dimension_semantics=("arbitrary",), vmem_limit_bytes=VMEM_LIMIT),
        name="conv_mlp",
    )(*operands2)
    return out
```

```python
import numpy as np
import jax
import jax.numpy as jnp
from jax import lax
from jax.experimental import pallas as pl
from jax.experimental.pallas import tpu as pltpu

D_MODEL = 1024
HEAD_DIM = 64
N_Q_HEADS = 16
N_KV_HEADS = 4
KV_WIDTH = N_KV_HEADS * HEAD_DIM
GROUP = N_Q_HEADS // N_KV_HEADS
BLOCK = 128
CONV_WIDTH = 31
D_FF = 4 * D_MODEL
N_BUCKETS = 32
MAX_DISTANCE = 128
EPS = 1e-6
NEG = -1e30
LOG2E = 1.4426950408889634

LANES = 128
SUBLANES = 8
N_LANE_BLOCKS = D_MODEL // LANES
COL_CHUNK = 2 * LANES
N_COL_CHUNKS = D_MODEL // COL_CHUNK
CONV_HALO = 32
CONV_SKIP = CONV_HALO - (CONV_WIDTH - 1)
CONV_ROWS = 64

MIX_TILE = 512
MLP_TILE = 512
FF_CHUNK = 1024
VMEM_LIMIT = 56 * 1024 * 1024

F32 = jnp.float32
BF16 = jnp.bfloat16


def _bucket_table():
    qi = np.arange(BLOCK, dtype=np.int32)[:, None]
    kj = np.arange(2 * BLOCK, dtype=np.int32)[None, :]
    dist = qi + BLOCK - kj
    n = np.maximum(dist, 0)
    max_exact = N_BUCKETS // 2
    nf = np.maximum(n, 1).astype(np.float32)
    large = max_exact + (np.log(nf / np.float32(max_exact)) / np.float32(np.log(MAX_DISTANCE / max_exact))
                         * np.float32(N_BUCKETS - max_exact)).astype(np.int32)
    large = np.minimum(large, N_BUCKETS - 1)
    bucket = np.where(n < max_exact, n, large)
    valid = (dist >= 0) & (dist < BLOCK)
    return np.where(valid, bucket, -1).astype(np.int32)


def _head_mean_matrix():
    idx = np.arange(COL_CHUNK) // HEAD_DIM
    return (idx[:, None] == idx[None, :]).astype(np.float32) / HEAD_DIM


def _dot(a, b):
    return jnp.dot(a, b, preferred_element_type=F32)


def _dot_nt(a, b):
    return lax.dot_general(a, b, (((1,), (1,)), ((), ())), preferred_element_type=F32)


def _rms(x, gain):
    return x * lax.rsqrt(jnp.mean(x * x, axis=-1, keepdims=True) + EPS) * gain


def _head_rms(t, hm, gain):
    ms = _dot((t * t).astype(BF16), hm)
    return t * lax.rsqrt(ms + EPS) * gain


def _mixer_kernel(x_ref, bucket_ref, relb_ref, sink_ref, hm_ref,
                  g_mix_ref, gq_ref, gk_ref, w_q_ref, w_k_ref, w_v_ref, w_a_ref, w_g_ref,
                  w_ga_ref, w_gc_ref, w_ao_ref, w_dw_ref, b_dw_ref, ln_g_ref, ln_b_ref,
                  w_co_ref, w_out_ref,
                  o_ref,
                  bias_s, u_s, q_s, kdup_s, vdup_s, attn_s, h_s, conv_s, ga_s, gc_s):
    tile = x_ref.shape[1]
    n_blocks = tile // BLOCK
    b = pl.program_id(0)
    j = pl.program_id(1)

    @pl.when((b == 0) & (j == 0))
    def _build_bias():
        bk = bucket_ref[...]
        left = lax.broadcasted_iota(jnp.int32, bk.shape, 1) < BLOCK
        for h in range(N_Q_HEADS):
            acc = jnp.full(bk.shape, NEG, F32)
            for bb in range(N_BUCKETS):
                acc = jnp.where(bk == bb, relb_ref[bb, h] * LOG2E, acc)
            bias_s[0, h] = acc
            bias_s[1, h] = jnp.where(left, NEG, acc)

    @pl.when(j == 0)
    def _zero_halo():
        kdup_s[:, 0:BLOCK, :] = jnp.zeros((N_KV_HEADS, BLOCK, LANES), BF16)
        vdup_s[:, 0:BLOCK, :] = jnp.zeros((N_KV_HEADS, BLOCK, LANES), BF16)
        h_s[:, 0:CONV_HALO, :] = jnp.zeros((N_LANE_BLOCKS, CONV_HALO, LANES), F32)

    @pl.when(j != 0)
    def _carry_halo():
        kdup_s[:, 0:BLOCK, :] = kdup_s[:, tile:tile + BLOCK, :]
        vdup_s[:, 0:BLOCK, :] = vdup_s[:, tile:tile + BLOCK, :]
        h_s[:, 0:CONV_HALO, :] = h_s[:, tile:tile + CONV_HALO, :]

    x = x_ref[0]
    u = _rms(x, g_mix_ref[...]).astype(BF16)
    hm = hm_ref[...]

    hglu = _dot(u, w_a_ref[...]) * jax.nn.sigmoid(_dot(u, w_g_ref[...]))
    for c in range(N_LANE_BLOCKS):
        h_s[c, CONV_HALO:CONV_HALO + tile, :] = hglu[:, c * LANES:(c + 1) * LANES]

    kn = _head_rms(_dot(u, w_k_ref[...]), hm, gk_ref[...])
    vv = _dot(u, w_v_ref[...])
    low = lax.broadcasted_iota(jnp.int32, (tile, LANES), 1) < HEAD_DIM
    for src, dst in ((kn, kdup_s), (vv, vdup_s)):
        for c in range(KV_WIDTH // LANES):
            s0 = src[:, c * LANES:(c + 1) * LANES]
            s1 = pltpu.roll(s0, HEAD_DIM, 1)
            dst[2 * c + 0, BLOCK:BLOCK + tile, :] = jnp.where(low, s0, s1).astype(BF16)
            dst[2 * c + 1, BLOCK:BLOCK + tile, :] = jnp.where(low, s1, s0).astype(BF16)
    u_s[...] = u

    def _conv_lane_block(c):
        for r in range(tile // CONV_ROWS):
            acc = jnp.broadcast_to(b_dw_ref[c], (CONV_ROWS, LANES))
            for a in range(SUBLANES):
                taps = range(a, CONV_WIDTH, SUBLANES)
                start = r * CONV_ROWS + CONV_SKIP + a
                rows = h_s[c, start:start + CONV_ROWS + SUBLANES * (len(taps) - 1), :]
                for k, t in enumerate(taps):
                    acc = acc + rows[SUBLANES * k:SUBLANES * k + CONV_ROWS] * w_dw_ref[c, t:t + 1, :]
            conv_s[c, r * CONV_ROWS:(r + 1) * CONV_ROWS, :] = acc

    def _conv_and_q(i, carry):
        _conv_lane_block(i)
        qn = _head_rms(_dot(u_s[...], w_q_ref[i]), hm, gq_ref[i])
        even = (lax.broadcasted_iota(jnp.int32, qn.shape, 1) & HEAD_DIM) == 0
        q_s[0, i] = jnp.where(even, qn, 0.0).astype(BF16)
        q_s[1, i] = jnp.where(even, 0.0, qn).astype(BF16)
        return carry

    def _conv_and_gates(i, carry):
        _conv_lane_block(i + N_COL_CHUNKS)
        ga_s[i] = jax.nn.sigmoid(_dot(u_s[...], w_ga_ref[i])).astype(BF16)
        gc_s[i] = jax.nn.sigmoid(_dot(u_s[...], w_gc_ref[i])).astype(BF16)
        return carry

    lax.fori_loop(0, N_COL_CHUNKS, _conv_and_q, 0)
    lax.fori_loop(0, N_COL_CHUNKS, _conv_and_gates, 0)

    def _attn_block(n, carry):
        r0 = pl.multiple_of(n * BLOCK, BLOCK)
        first = jnp.where((j == 0) & (n == 0), 1, 0)
        low_b = lax.broadcasted_iota(jnp.int32, (BLOCK, LANES), 1) < HEAD_DIM
        for g in range(N_KV_HEADS):
            q4 = jnp.concatenate(
                [q_s[i, g, pl.ds(r0, BLOCK), pr * LANES:(pr + 1) * LANES] for pr in range(2) for i in range(2)],
                axis=0)
            kb = kdup_s[g, pl.ds(r0, 2 * BLOCK), :]
            vb = vdup_s[g, pl.ds(r0, 2 * BLOCK), :]
            s4 = _dot_nt(q4, kb)
            es, rs = [], []
            for hh in range(GROUP):
                h = GROUP * g + hh
                s = s4[hh * BLOCK:(hh + 1) * BLOCK] + bias_s[first, h]
                m = jnp.max(s, axis=-1, keepdims=True)
                e = jnp.exp2(s - m)
                denom = jnp.sum(e, axis=-1, keepdims=True) + jnp.exp2(sink_ref[h] * LOG2E - m)
                es.append(e.astype(BF16))
                rs.append(1.0 / denom)
            o4 = _dot(jnp.concatenate(es, axis=0), vb)
            for pr in range(2):
                o_even = o4[(2 * pr) * BLOCK:(2 * pr + 1) * BLOCK] * rs[2 * pr]
                o_odd = o4[(2 * pr + 1) * BLOCK:(2 * pr + 2) * BLOCK] * rs[2 * pr + 1]
                p = 2 * g + pr
                attn_s[pl.ds(r0, BLOCK), p * LANES:(p + 1) * LANES] = jnp.where(low_b, o_even, o_odd).astype(BF16)
        return carry

    lax.fori_loop(0, n_blocks, _attn_block, 0)

    cv = jnp.concatenate([conv_s[c] for c in range(N_LANE_BLOCKS)], axis=1)
    mu = jnp.mean(cv, axis=-1, keepdims=True)
    xc = cv - mu
    var = jnp.mean(xc * xc, axis=-1, keepdims=True)
    y = xc * lax.rsqrt(var + EPS) * ln_g_ref[...] + ln_b_ref[...]
    conv = _dot((y * jax.nn.sigmoid(y)).astype(BF16), w_co_ref[...])

    attn = _dot(attn_s[...], w_ao_ref[...])
    gate_a = jnp.concatenate([ga_s[i] for i in range(N_COL_CHUNKS)], axis=1).astype(F32)
    gate_c = jnp.concatenate([gc_s[i] for i in range(N_COL_CHUNKS)], axis=1).astype(F32)
    merged = gate_a * attn + gate_c * conv
    o_ref[0] = x + _dot(merged.astype(BF16), w_out_ref[...])


def _mlp_kernel(x_ref, g_ref, w1_ref, w2_ref, o_ref, xn_s):
    x = x_ref[...]
    xn_s[...] = _rms(x, g_ref[...]).astype(BF16)
    acc = x
    for c in range(D_FF // FF_CHUNK):
        t = jnp.maximum(_dot(xn_s[...], w1_ref[c]).astype(BF16), 0.0)
        acc = acc + _dot(t * t, w2_ref[c])
    o_ref[...] = acc


def _resident(shape):
    zeros = (0,) * len(shape)
    return pl.BlockSpec(shape, lambda *_: zeros, pipeline_mode=pl.Buffered(1))


def _smem():
    return pl.BlockSpec(memory_space=pltpu.SMEM)


def kernel(x, norm_mix_g, w_in, q_norm_g, k_norm_g, attn_sinks, rel_bias, w_attn_o, w_dw, b_dw,
           conv_ln_g, conv_ln_b, w_conv_out, w_out, norm_mlp_g, w_ff1, w_ff2):
    batch, seq, d = x.shape
    assert d == D_MODEL and seq % MIX_TILE == 0 and (batch * seq) % MLP_TILE == 0
    layer = 0
    row = lambda v: v.reshape(1, -1).astype(F32)
    w = w_in[layer].astype(BF16)
    q_end = N_Q_HEADS * HEAD_DIM
    k_end = q_end + KV_WIDTH
    v_end = k_end + KV_WIDTH
    a_end = v_end + D_MODEL
    g_end = a_end + D_MODEL
    ga_end = g_end + D_MODEL
    lane_blocks = lambda v: v.astype(F32).reshape(-1, N_LANE_BLOCKS, LANES).transpose(1, 0, 2)
    col_chunks = lambda m: m.reshape(m.shape[0], N_COL_CHUNKS, COL_CHUNK).transpose(1, 0, 2)
    q_gain = jnp.tile(q_norm_g[layer].astype(F32), N_Q_HEADS) * (HEAD_DIM ** -0.5 * LOG2E)

    operands = [
        x,
        jnp.asarray(_bucket_table()),
        rel_bias.astype(F32),
        attn_sinks[layer].astype(F32),
        jnp.asarray(_head_mean_matrix(), dtype=BF16),
        row(norm_mix_g[layer]),
        q_gain.reshape(N_COL_CHUNKS, 1, COL_CHUNK),
        row(jnp.tile(k_norm_g[layer], N_KV_HEADS)),
        col_chunks(w[:, :q_end]), w[:, q_end:k_end], w[:, k_end:v_end], w[:, v_end:a_end], w[:, a_end:g_end],
        col_chunks(w[:, g_end:ga_end]), col_chunks(w[:, ga_end:]),
        w_attn_o[layer].astype(BF16),
        lane_blocks(w_dw[layer]),
        lane_blocks(b_dw[layer].reshape(1, -1)),
        row(conv_ln_g[layer]), row(conv_ln_b[layer]),
        w_conv_out[layer].astype(BF16),
        w_out[layer].astype(BF16),
    ]
    in_specs = [pl.BlockSpec((1, MIX_TILE, D_MODEL), lambda b, j: (b, j, 0)),
                _resident(operands[1].shape), _smem(), _smem()]
    in_specs += [_resident(op.shape) for op in operands[4:]]

    x1 = pl.pallas_call(
        _mixer_kernel,
        grid=(batch, seq // MIX_TILE),
        in_specs=in_specs,
        out_specs=pl.BlockSpec((1, MIX_TILE, D_MODEL), lambda b, j: (b, j, 0)),
        out_shape=jax.ShapeDtypeStruct(x.shape, F32),
        scratch_shapes=[
            pltpu.VMEM((2, N_Q_HEADS, BLOCK, 2 * BLOCK), F32),
            pltpu.VMEM((MIX_TILE, D_MODEL), BF16),
            pltpu.VMEM((2, N_COL_CHUNKS, MIX_TILE, COL_CHUNK), BF16),
            pltpu.VMEM((N_KV_HEADS, BLOCK + MIX_TILE, LANES), BF16),
            pltpu.VMEM((N_KV_HEADS, BLOCK + MIX_TILE, LANES), BF16),
            pltpu.VMEM((MIX_TILE, D_MODEL), BF16),
            pltpu.VMEM((N_LANE_BLOCKS, CONV_HALO + MIX_TILE, LANES), F32),
            pltpu.VMEM((N_LANE_BLOCKS, MIX_TILE, LANES), F32),
            pltpu.VMEM((N_COL_CHUNKS, MIX_TILE, COL_CHUNK), BF16),
            pltpu.VMEM((N_COL_CHUNKS, MIX_TILE, COL_CHUNK), BF16),
        ],
        compiler_params=pltpu.CompilerParams(
            dimension_semantics=("arbitrary", "arbitrary"), vmem_limit_bytes=VMEM_LIMIT),
        name="mixer",
    )(*operands)

    tokens = batch * seq
    out = pl.pallas_call(
        _mlp_kernel,
        grid=(tokens // MLP_TILE,),
        in_specs=[pl.BlockSpec((MLP_TILE, D_MODEL), lambda i: (i, 0)),
                  _resident((1, D_MODEL)), _resident((D_FF // FF_CHUNK, D_MODEL, FF_CHUNK)),
                  _resident((D_FF // FF_CHUNK, FF_CHUNK, D_MODEL))],
        out_specs=pl.BlockSpec((MLP_TILE, D_MODEL), lambda i: (i, 0)),
        out_shape=jax.ShapeDtypeStruct((tokens, D_MODEL), F32),
        scratch_shapes=[pltpu.VMEM((MLP_TILE, D_MODEL), BF16)],
        compiler_params=pltpu.CompilerParams(
            dimension_semantics=("arbitrary",), vmem_limit_bytes=VMEM_LIMIT),
        name="mlp",
    )(x1.reshape(tokens, D_MODEL), row(norm_mlp_g[layer]),
      w_ff1[layer].astype(BF16).reshape(D_MODEL, D_FF // FF_CHUNK, FF_CHUNK).transpose(1, 0, 2),
      w_ff2[layer].astype(BF16).reshape(D_FF // FF_CHUNK, FF_CHUNK, D_MODEL))
    return out.reshape(batch, seq, D_MODEL)
```

```python
import functools

import numpy as np
import jax
import jax.numpy as jnp
from jax import lax
from jax.experimental import pallas as pl
from jax.experimental.pallas import tpu as pltpu

D_MODEL = 1024
HEAD_DIM = 64
N_Q_HEADS = 16
N_KV_HEADS = 4
KV_WIDTH = N_KV_HEADS * HEAD_DIM
GROUP = N_Q_HEADS // N_KV_HEADS
BLOCK = 128
CONV_WIDTH = 31
D_FF = 4 * D_MODEL
N_BUCKETS = 32
MAX_DISTANCE = 128
EPS = 1e-6
NEG = -1e30
LOG2E = 1.4426950408889634
QK_BOUND_MARGIN = 1.03
MAX_EXP2_SPREAD = 100.0

LANES = 128
SUBLANES = 8
N_LANE_BLOCKS = D_MODEL // LANES
COL_CHUNK = 2 * LANES
N_COL_CHUNKS = D_MODEL // COL_CHUNK
CONV_HALO = 32
CONV_SKIP = CONV_HALO - (CONV_WIDTH - 1)
CONV_ROWS = 64

MIX_TILE = 512
MLP_TILE = 512
FF_CHUNK = 1024
VMEM_LIMIT = 56 * 1024 * 1024

F32 = jnp.float32
BF16 = jnp.bfloat16


def _bucket_table():
    qi = np.arange(BLOCK, dtype=np.int32)[:, None]
    kj = np.arange(2 * BLOCK, dtype=np.int32)[None, :]
    dist = qi + BLOCK - kj
    n = np.maximum(dist, 0)
    max_exact = N_BUCKETS // 2
    nf = np.maximum(n, 1).astype(np.float32)
    large = max_exact + (np.log(nf / np.float32(max_exact)) / np.float32(np.log(MAX_DISTANCE / max_exact))
                         * np.float32(N_BUCKETS - max_exact)).astype(np.int32)
    large = np.minimum(large, N_BUCKETS - 1)
    bucket = np.where(n < max_exact, n, large)
    valid = (dist >= 0) & (dist < BLOCK)
    return np.where(valid, bucket, -1).astype(np.int32)


def _head_mean_matrix():
    idx = np.arange(COL_CHUNK) // HEAD_DIM
    return (idx[:, None] == idx[None, :]).astype(np.float32) / HEAD_DIM


def _dot(a, b):
    return jnp.dot(a, b, preferred_element_type=F32)


def _dot_nt(a, b):
    return lax.dot_general(a, b, (((1,), (1,)), ((), ())), preferred_element_type=F32)


def _rms(x, gain):
    return x * lax.rsqrt(jnp.mean(x * x, axis=-1, keepdims=True) + EPS) * gain


def _head_rms(t, hm, gain):
    ms = _dot((t * t).astype(BF16), hm)
    return t * lax.rsqrt(ms + EPS) * gain


def _mixer_kernel(x_ref, bucket_ref, relb_ref, sink_ref, qk_bound_ref, hm_ref,
                  g_mix_ref, gq_ref, gk_ref, w_q_ref, w_k_ref, w_v_ref, w_a_ref, w_g_ref,
                  w_ga_ref, w_gc_ref, w_ao_ref, w_dw_ref, b_dw_ref, ln_g_ref, ln_b_ref,
                  w_co_ref, w_out_ref,
                  o_ref,
                  bias_s, sink_s, bounded_s, u_s, q_s, kdup_s, vdup_s, attn_s, h_s, conv_s, ga_s, gc_s):
    tile = x_ref.shape[1]
    n_blocks = tile // BLOCK
    b = pl.program_id(0)
    j = pl.program_id(1)

    @pl.when((b == 0) & (j == 0))
    def _build_bias():
        qk_bound = qk_bound_ref[0]
        his = [relb_ref[0, h] for h in range(N_Q_HEADS)]
        los = list(his)
        for bb in range(1, N_BUCKETS):
            his = [jnp.maximum(hi, relb_ref[bb, h]) for h, hi in enumerate(his)]
            los = [jnp.minimum(lo, relb_ref[bb, h]) for h, lo in enumerate(los)]
        spread = (his[0] - los[0]) * LOG2E
        for h in range(1, N_Q_HEADS):
            spread = jnp.maximum(spread, (his[h] - los[h]) * LOG2E)
        bounded = 2.0 * qk_bound + spread <= MAX_EXP2_SPREAD
        bounded_s[0] = bounded.astype(jnp.int32)

        bk = bucket_ref[...]
        left = lax.broadcasted_iota(jnp.int32, bk.shape, 1) < BLOCK
        for h in range(N_Q_HEADS):
            shift = jnp.where(bounded, qk_bound + his[h] * LOG2E, 0.0)
            acc = jnp.full(bk.shape, NEG, F32)
            for bb in range(N_BUCKETS):
                acc = jnp.where(bk == bb, relb_ref[bb, h] * LOG2E - shift, acc)
            bias_s[0, h] = acc
            bias_s[1, h] = jnp.where(left, NEG, acc)
            sink_s[h] = jnp.max(jnp.exp2(jnp.full((SUBLANES, LANES), sink_ref[h] * LOG2E - shift, F32)))

    @pl.when(j == 0)
    def _zero_halo():
        kdup_s[:, 0:BLOCK, :] = jnp.zeros((N_KV_HEADS, BLOCK, LANES), BF16)
        vdup_s[:, 0:BLOCK, :] = jnp.zeros((N_KV_HEADS, BLOCK, LANES), BF16)
        h_s[:, 0:CONV_HALO, :] = jnp.zeros((N_LANE_BLOCKS, CONV_HALO, LANES), F32)

    @pl.when(j != 0)
    def _carry_halo():
        kdup_s[:, 0:BLOCK, :] = kdup_s[:, tile:tile + BLOCK, :]
        vdup_s[:, 0:BLOCK, :] = vdup_s[:, tile:tile + BLOCK, :]
        h_s[:, 0:CONV_HALO, :] = h_s[:, tile:tile + CONV_HALO, :]

    x = x_ref[0]
    u = _rms(x, g_mix_ref[...]).astype(BF16)
    hm = hm_ref[...]

    hglu = _dot(u, w_a_ref[...]) * jax.nn.sigmoid(_dot(u, w_g_ref[...]))
    for c in range(N_LANE_BLOCKS):
        h_s[c, CONV_HALO:CONV_HALO + tile, :] = hglu[:, c * LANES:(c + 1) * LANES]

    kn = _head_rms(_dot(u, w_k_ref[...]), hm, gk_ref[...])
    vv = _dot(u, w_v_ref[...])
    low = lax.broadcasted_iota(jnp.int32, (tile, LANES), 1) < HEAD_DIM
    for src, dst in ((kn, kdup_s), (vv, vdup_s)):
        for c in range(KV_WIDTH // LANES):
            s0 = src[:, c * LANES:(c + 1) * LANES]
            s1 = pltpu.roll(s0, HEAD_DIM, 1)
            dst[2 * c + 0, BLOCK:BLOCK + tile, :] = jnp.where(low, s0, s1).astype(BF16)
            dst[2 * c + 1, BLOCK:BLOCK + tile, :] = jnp.where(low, s1, s0).astype(BF16)
    u_s[...] = u

    def _conv_lane_block(c):
        for r in range(tile // CONV_ROWS):
            acc = jnp.broadcast_to(b_dw_ref[c], (CONV_ROWS, LANES))
            for a in range(SUBLANES):
                taps = range(a, CONV_WIDTH, SUBLANES)
                start = r * CONV_ROWS + CONV_SKIP + a
                rows = h_s[c, start:start + CONV_ROWS + SUBLANES * (len(taps) - 1), :]
                for k, t in enumerate(taps):
                    acc = acc + rows[SUBLANES * k:SUBLANES * k + CONV_ROWS] * w_dw_ref[c, t:t + 1, :]
            conv_s[c, r * CONV_ROWS:(r + 1) * CONV_ROWS, :] = acc

    def _conv_and_q(i, carry):
        _conv_lane_block(i)
        qn = _head_rms(_dot(u_s[...], w_q_ref[i]), hm, gq_ref[i])
        even = (lax.broadcasted_iota(jnp.int32, qn.shape, 1) & HEAD_DIM) == 0
        q_s[0, i] = jnp.where(even, qn, 0.0).astype(BF16)
        q_s[1, i] = jnp.where(even, 0.0, qn).astype(BF16)
        return carry

    def _conv_and_gates(i, carry):
        _conv_lane_block(i + N_COL_CHUNKS)
        ga_s[i] = jax.nn.sigmoid(_dot(u_s[...], w_ga_ref[i])).astype(BF16)
        gc_s[i] = jax.nn.sigmoid(_dot(u_s[...], w_gc_ref[i])).astype(BF16)
        return carry

    lax.fori_loop(0, N_COL_CHUNKS, _conv_and_q, 0)
    lax.fori_loop(0, N_COL_CHUNKS, _conv_and_gates, 0)

    def _attn_block(bounded, n, carry):
        r0 = pl.multiple_of(n * BLOCK, BLOCK)
        first = jnp.where((j == 0) & (n == 0), 1, 0)
        low_b = lax.broadcasted_iota(jnp.int32, (BLOCK, LANES), 1) < HEAD_DIM
        for g in range(N_KV_HEADS):
            q4 = jnp.concatenate(
                [q_s[i, g, pl.ds(r0, BLOCK), pr * LANES:(pr + 1) * LANES] for pr in range(2) for i in range(2)],
                axis=0)
            kb = kdup_s[g, pl.ds(r0, 2 * BLOCK), :]
            vb = vdup_s[g, pl.ds(r0, 2 * BLOCK), :]
            s4 = _dot_nt(q4, kb)
            es, rs = [], []
            for hh in range(GROUP):
                h = GROUP * g + hh
                s = s4[hh * BLOCK:(hh + 1) * BLOCK] + bias_s[first, h]
                if bounded:
                    e = jnp.exp2(s)
                    denom = jnp.sum(e, axis=-1, keepdims=True) + sink_s[h]
                else:
                    m = jnp.max(s, axis=-1, keepdims=True)
                    e = jnp.exp2(s - m)
                    denom = jnp.sum(e, axis=-1, keepdims=True) + jnp.exp2(sink_ref[h] * LOG2E - m)
                es.append(e.astype(BF16))
                rs.append(1.0 / denom)
            o4 = _dot(jnp.concatenate(es, axis=0), vb)
            for pr in range(2):
                o_even = o4[(2 * pr) * BLOCK:(2 * pr + 1) * BLOCK] * rs[2 * pr]
                o_odd = o4[(2 * pr + 1) * BLOCK:(2 * pr + 2) * BLOCK] * rs[2 * pr + 1]
                p = 2 * g + pr
                attn_s[pl.ds(r0, BLOCK), p * LANES:(p + 1) * LANES] = jnp.where(low_b, o_even, o_odd).astype(BF16)
        return carry

    @pl.when(bounded_s[0] == 1)
    def _attention_constant_shift():
        lax.fori_loop(0, n_blocks, functools.partial(_attn_block, True), 0)

    @pl.when(bounded_s[0] != 1)
    def _attention_row_max():
        lax.fori_loop(0, n_blocks, functools.partial(_attn_block, False), 0)

    cv = jnp.concatenate([conv_s[c] for c in range(N_LANE_BLOCKS)], axis=1)
    mu = jnp.mean(cv, axis=-1, keepdims=True)
    xc = cv - mu
    var = jnp.mean(xc * xc, axis=-1, keepdims=True)
    y = xc * lax.rsqrt(var + EPS) * ln_g_ref[...] + ln_b_ref[...]
    conv = _dot((y * jax.nn.sigmoid(y)).astype(BF16), w_co_ref[...])

    attn = _dot(attn_s[...], w_ao_ref[...])
    gate_a = jnp.concatenate([ga_s[i] for i in range(N_COL_CHUNKS)], axis=1).astype(F32)
    gate_c = jnp.concatenate([gc_s[i] for i in range(N_COL_CHUNKS)], axis=1).astype(F32)
    merged = gate_a * attn + gate_c * conv
    o_ref[0] = x + _dot(merged.astype(BF16), w_out_ref[...])


def _mlp_kernel(x_ref, g_ref, w1_ref, w2_ref, o_ref, xn_s):
    x = x_ref[...]
    xn_s[...] = _rms(x, g_ref[...]).astype(BF16)
    acc = x
    for c in range(D_FF // FF_CHUNK):
        hmid = jnp.square(jnp.maximum(_dot(xn_s[...], w1_ref[c]), 0.0))
        acc = acc + _dot(hmid.astype(BF16), w2_ref[c])
    o_ref[...] = acc


def _resident(shape):
    zeros = (0,) * len(shape)
    return pl.BlockSpec(shape, lambda *_: zeros, pipeline_mode=pl.Buffered(1))


def _smem():
    return pl.BlockSpec(memory_space=pltpu.SMEM)


def kernel(x, norm_mix_g, w_in, q_norm_g, k_norm_g, attn_sinks, rel_bias, w_attn_o, w_dw, b_dw,
           conv_ln_g, conv_ln_b, w_conv_out, w_out, norm_mlp_g, w_ff1, w_ff2):
    batch, seq, d = x.shape
    assert d == D_MODEL and seq % MIX_TILE == 0 and (batch * seq) % MLP_TILE == 0
    layer = 0
    row = lambda v: v.reshape(1, -1).astype(F32)
    w = w_in[layer].astype(BF16)
    q_end = N_Q_HEADS * HEAD_DIM
    k_end = q_end + KV_WIDTH
    v_end = k_end + KV_WIDTH
    a_end = v_end + D_MODEL
    g_end = a_end + D_MODEL
    ga_end = g_end + D_MODEL
    lane_blocks = lambda v: v.astype(F32).reshape(-1, N_LANE_BLOCKS, LANES).transpose(1, 0, 2)
    col_chunks = lambda m: m.reshape(m.shape[0], N_COL_CHUNKS, COL_CHUNK).transpose(1, 0, 2)
    q_gain = jnp.tile(q_norm_g[layer].astype(F32), N_Q_HEADS) * (HEAD_DIM ** -0.5 * LOG2E)
    qk_bound = (QK_BOUND_MARGIN * HEAD_DIM * jnp.max(jnp.abs(q_gain))
                * jnp.max(jnp.abs(k_norm_g[layer].astype(F32))))

    operands = [
        x,
        jnp.asarray(_bucket_table()),
        rel_bias.astype(F32),
        attn_sinks[layer].astype(F32),
        qk_bound.reshape(1),
        jnp.asarray(_head_mean_matrix(), dtype=BF16),
        row(norm_mix_g[layer]),
        q_gain.reshape(N_COL_CHUNKS, 1, COL_CHUNK),
        row(jnp.tile(k_norm_g[layer], N_KV_HEADS)),
        col_chunks(w[:, :q_end]), w[:, q_end:k_end], w[:, k_end:v_end], w[:, v_end:a_end], w[:, a_end:g_end],
        col_chunks(w[:, g_end:ga_end]), col_chunks(w[:, ga_end:]),
        w_attn_o[layer].astype(BF16),
        lane_blocks(w_dw[layer]),
        lane_blocks(b_dw[layer].reshape(1, -1)),
        row(conv_ln_g[layer]), row(conv_ln_b[layer]),
        w_conv_out[layer].astype(BF16),
        w_out[layer].astype(BF16),
    ]
    in_specs = [pl.BlockSpec((1, MIX_TILE, D_MODEL), lambda b, j: (b, j, 0)),
                _resident(operands[1].shape), _smem(), _smem(), _smem()]
    in_specs += [_resident(op.shape) for op in operands[5:]]

    x1 = pl.pallas_call(
        _mixer_kernel,
        grid=(batch, seq // MIX_TILE),
        in_specs=in_specs,
        out_specs=pl.BlockSpec((1, MIX_TILE, D_MODEL), lambda b, j: (b, j, 0)),
        out_shape=jax.ShapeDtypeStruct(x.shape, F32),
        scratch_shapes=[
            pltpu.VMEM((2, N_Q_HEADS, BLOCK, 2 * BLOCK), F32),
            pltpu.SMEM((N_Q_HEADS,), F32),
            pltpu.SMEM((1,), jnp.int32),
            pltpu.VMEM((MIX_TILE, D_MODEL), BF16),
            pltpu.VMEM((2, N_COL_CHUNKS, MIX_TILE, COL_CHUNK), BF16),
            pltpu.VMEM((N_KV_HEADS, BLOCK + MIX_TILE, LANES), BF16),
            pltpu.VMEM((N_KV_HEADS, BLOCK + MIX_TILE, LANES), BF16),
            pltpu.VMEM((MIX_TILE, D_MODEL), BF16),
            pltpu.VMEM((N_LANE_BLOCKS, CONV_HALO + MIX_TILE, LANES), F32),
            pltpu.VMEM((N_LANE_BLOCKS, MIX_TILE, LANES), F32),
            pltpu.VMEM((N_COL_CHUNKS, MIX_TILE, COL_CHUNK), BF16),
            pltpu.VMEM((N_COL_CHUNKS, MIX_TILE, COL_CHUNK), BF16),
        ],
        compiler_params=pltpu.CompilerParams(
            dimension_semantics=("arbitrary", "arbitrary"), vmem_limit_bytes=VMEM_LIMIT),
        name="mixer",
    )(*operands)

    tokens = batch * seq
    out = pl.pallas_call(
        _mlp_kernel,
        grid=(tokens // MLP_TILE,),
        in_specs=[pl.BlockSpec((MLP_TILE, D_MODEL), lambda i: (i, 0)),
                  _resident((1, D_MODEL)), _resident((D_FF // FF_CHUNK, D_MODEL, FF_CHUNK)),
                  _resident((D_FF // FF_CHUNK, FF_CHUNK, D_MODEL))],
        out_specs=pl.BlockSpec((MLP_TILE, D_MODEL), lambda i: (i, 0)),
        out_shape=jax.ShapeDtypeStruct((tokens, D_MODEL), F32),
        scratch_shapes=[pltpu.VMEM((MLP_TILE, D_MODEL), BF16)],
        compiler_params=pltpu.CompilerParams(
            dimension_semantics=("arbitrary",), vmem_limit_bytes=VMEM_LIMIT),
        name="mlp",
    )(x1.reshape(tokens, D_MODEL), row(norm_mlp_g[layer]),
      w_ff1[layer].astype(BF16).reshape(D_MODEL, D_FF // FF_CHUNK, FF_CHUNK).transpose(1, 0, 2),
      w_ff2[layer].astype(BF16).reshape(D_FF // FF_CHUNK, FF_CHUNK, D_MODEL))
    return out.reshape(batch, seq, D_MODEL)
```

```python
import functools

import numpy as np
import jax
import jax.numpy as jnp
from jax import lax
from jax.experimental import pallas as pl
from jax.experimental.pallas import tpu as pltpu

D_MODEL = 1024
HEAD_DIM = 64
N_Q_HEADS = 16
N_KV_HEADS = 4
KV_WIDTH = N_KV_HEADS * HEAD_DIM
GROUP = N_Q_HEADS // N_KV_HEADS
BLOCK = 128
CONV_WIDTH = 31
D_FF = 4 * D_MODEL
N_BUCKETS = 32
MAX_DISTANCE = 128
EPS = 1e-6
NEG = -1e30
LOG2E = 1.4426950408889634
QK_BOUND_MARGIN = 1.03
MAX_EXP2_SPREAD = 100.0

LANES = 128
SUBLANES = 8
N_LANE_BLOCKS = D_MODEL // LANES
COL_CHUNK = 2 * LANES
N_COL_CHUNKS = D_MODEL // COL_CHUNK
CONV_HALO = 32
CONV_SKIP = CONV_HALO - (CONV_WIDTH - 1)
CONV_ROWS = 64

MIX_TILE = 512
MLP_TILE = 512
FF_CHUNK = 1024
VMEM_LIMIT = 56 * 1024 * 1024

F32 = jnp.float32
BF16 = jnp.bfloat16


def _bucket_table():
    qi = np.arange(BLOCK, dtype=np.int32)[:, None]
    kj = np.arange(2 * BLOCK, dtype=np.int32)[None, :]
    dist = qi + BLOCK - kj
    n = np.maximum(dist, 0)
    max_exact = N_BUCKETS // 2
    nf = np.maximum(n, 1).astype(np.float32)
    large = max_exact + (np.log(nf / np.float32(max_exact)) / np.float32(np.log(MAX_DISTANCE / max_exact))
                         * np.float32(N_BUCKETS - max_exact)).astype(np.int32)
    large = np.minimum(large, N_BUCKETS - 1)
    bucket = np.where(n < max_exact, n, large)
    valid = (dist >= 0) & (dist < BLOCK)
    return np.where(valid, bucket, -1).astype(np.int32)


def _head_mean_matrix():
    idx = np.arange(COL_CHUNK) // HEAD_DIM
    return (idx[:, None] == idx[None, :]).astype(np.float32) / HEAD_DIM


def _dot(a, b):
    return jnp.dot(a, b, preferred_element_type=F32)


def _dot_nt(a, b):
    return lax.dot_general(a, b, (((1,), (1,)), ((), ())), preferred_element_type=F32)


def _rms(x, gain):
    return x * lax.rsqrt(jnp.mean(x * x, axis=-1, keepdims=True) + EPS) * gain


def _head_rms(t, hm, gain):
    ms = _dot((t * t).astype(BF16), hm)
    return t * lax.rsqrt(ms + EPS) * gain


def _mixer_kernel(x_ref, bucket_ref, relb_ref, sink_ref, qk_bound_ref, hm_ref,
                  g_mix_ref, gq_ref, gk_ref, w_q_ref, w_kv_ref, w_a_ref, w_g_ref,
                  w_ga_ref, w_gc_ref, w_ao_ref, w_dw_ref, b_dw_ref, ln_g_ref, ln_b_ref,
                  w_co_ref, w_out_ref,
                  o_ref,
                  bias_s, sink_s, bounded_s, u_s, q_s, kdup_s, vdup_s, attn_s, h_s, conv_s, ga_s, gc_s):
    tile = x_ref.shape[1]
    n_blocks = tile // BLOCK
    b = pl.program_id(0)
    j = pl.program_id(1)

    @pl.when((b == 0) & (j == 0))
    def _build_bias():
        qk_bound = qk_bound_ref[0]
        his = [relb_ref[0, h] for h in range(N_Q_HEADS)]
        los = list(his)
        for bb in range(1, N_BUCKETS):
            his = [jnp.maximum(hi, relb_ref[bb, h]) for h, hi in enumerate(his)]
            los = [jnp.minimum(lo, relb_ref[bb, h]) for h, lo in enumerate(los)]
        spread = (his[0] - los[0]) * LOG2E
        for h in range(1, N_Q_HEADS):
            spread = jnp.maximum(spread, (his[h] - los[h]) * LOG2E)
        bounded = 2.0 * qk_bound + spread <= MAX_EXP2_SPREAD
        bounded_s[0] = bounded.astype(jnp.int32)

        bk = bucket_ref[...]
        left = lax.broadcasted_iota(jnp.int32, bk.shape, 1) < BLOCK
        for h in range(N_Q_HEADS):
            shift = jnp.where(bounded, qk_bound + his[h] * LOG2E, 0.0)
            acc = jnp.full(bk.shape, NEG, F32)
            for bb in range(N_BUCKETS):
                acc = jnp.where(bk == bb, relb_ref[bb, h] * LOG2E - shift, acc)
            bias_s[0, h] = acc
            bias_s[1, h] = jnp.where(left, NEG, acc)
            sink_s[h] = jnp.max(jnp.exp2(jnp.full((SUBLANES, LANES), sink_ref[h] * LOG2E - shift, F32)))

    @pl.when(j == 0)
    def _zero_halo():
        kdup_s[:, 0:BLOCK, :] = jnp.zeros((N_KV_HEADS, BLOCK, LANES), BF16)
        vdup_s[:, 0:BLOCK, :] = jnp.zeros((N_KV_HEADS, BLOCK, LANES), BF16)
        h_s[:, 0:CONV_HALO, :] = jnp.zeros((N_LANE_BLOCKS, CONV_HALO, LANES), F32)

    @pl.when(j != 0)
    def _carry_halo():
        kdup_s[:, 0:BLOCK, :] = kdup_s[:, tile:tile + BLOCK, :]
        vdup_s[:, 0:BLOCK, :] = vdup_s[:, tile:tile + BLOCK, :]
        h_s[:, 0:CONV_HALO, :] = h_s[:, tile:tile + CONV_HALO, :]

    x = x_ref[0]
    u = _rms(x, g_mix_ref[...]).astype(BF16)
    hm = hm_ref[...]

    hglu = _dot(u, w_a_ref[...]) * jax.nn.sigmoid(_dot(u, w_g_ref[...]))
    for c in range(N_LANE_BLOCKS):
        h_s[c, CONV_HALO:CONV_HALO + tile, :] = hglu[:, c * LANES:(c + 1) * LANES]

    kv = _dot(u, w_kv_ref[...])
    kn = _head_rms(kv[:, :KV_WIDTH], hm, gk_ref[...])
    vv = kv[:, KV_WIDTH:]
    low = lax.broadcasted_iota(jnp.int32, (tile, LANES), 1) < HEAD_DIM
    for src, dst in ((kn, kdup_s), (vv, vdup_s)):
        for c in range(KV_WIDTH // LANES):
            s0 = src[:, c * LANES:(c + 1) * LANES]
            s1 = pltpu.roll(s0, HEAD_DIM, 1)
            dst[2 * c + 0, BLOCK:BLOCK + tile, :] = jnp.where(low, s0, s1).astype(BF16)
            dst[2 * c + 1, BLOCK:BLOCK + tile, :] = jnp.where(low, s1, s0).astype(BF16)
    u_s[...] = u

    def _conv_lane_block(c):
        for r in range(tile // CONV_ROWS):
            acc = jnp.broadcast_to(b_dw_ref[c], (CONV_ROWS, LANES))
            for a in range(SUBLANES):
                taps = range(a, CONV_WIDTH, SUBLANES)
                start = r * CONV_ROWS + CONV_SKIP + a
                rows = h_s[c, start:start + CONV_ROWS + SUBLANES * (len(taps) - 1), :]
                for k, t in enumerate(taps):
                    acc = acc + rows[SUBLANES * k:SUBLANES * k + CONV_ROWS] * w_dw_ref[c, t:t + 1, :]
            conv_s[c, r * CONV_ROWS:(r + 1) * CONV_ROWS, :] = acc

    def _conv_and_q(i, carry):
        _conv_lane_block(i)
        qn = _head_rms(_dot(u_s[...], w_q_ref[i]), hm, gq_ref[i])
        even = (lax.broadcasted_iota(jnp.int32, qn.shape, 1) & HEAD_DIM) == 0
        q_s[0, i] = jnp.where(even, qn, 0.0).astype(BF16)
        q_s[1, i] = jnp.where(even, 0.0, qn).astype(BF16)
        return carry

    def _conv_and_gates(i, carry):
        _conv_lane_block(i + N_COL_CHUNKS)
        ga_s[i] = jax.nn.sigmoid(_dot(u_s[...], w_ga_ref[i])).astype(BF16)
        gc_s[i] = jax.nn.sigmoid(_dot(u_s[...], w_gc_ref[i])).astype(BF16)
        return carry

    lax.fori_loop(0, N_COL_CHUNKS, _conv_and_q, 0)
    lax.fori_loop(0, N_COL_CHUNKS, _conv_and_gates, 0)

    def _attn_block(bounded, n, carry):
        r0 = pl.multiple_of(n * BLOCK, BLOCK)
        first = jnp.where((j == 0) & (n == 0), 1, 0)
        low_b = lax.broadcasted_iota(jnp.int32, (BLOCK, LANES), 1) < HEAD_DIM
        for g in range(N_KV_HEADS):
            q4 = jnp.concatenate(
                [q_s[i, g, pl.ds(r0, BLOCK), pr * LANES:(pr + 1) * LANES] for pr in range(2) for i in range(2)],
                axis=0)
            kb = kdup_s[g, pl.ds(r0, 2 * BLOCK), :]
            vb = vdup_s[g, pl.ds(r0, 2 * BLOCK), :]
            s4 = _dot_nt(q4, kb)
            es, rs = [], []
            for hh in range(GROUP):
                h = GROUP * g + hh
                s = s4[hh * BLOCK:(hh + 1) * BLOCK] + bias_s[first, h]
                if bounded:
                    e = jnp.exp2(s)
                    denom = jnp.sum(e, axis=-1, keepdims=True) + sink_s[h]
                else:
                    m = jnp.max(s, axis=-1, keepdims=True)
                    e = jnp.exp2(s - m)
                    denom = jnp.sum(e, axis=-1, keepdims=True) + jnp.exp2(sink_ref[h] * LOG2E - m)
                es.append(e.astype(BF16))
                rs.append(1.0 / denom)
            o4 = _dot(jnp.concatenate(es, axis=0), vb)
            for pr in range(2):
                o_even = o4[(2 * pr) * BLOCK:(2 * pr + 1) * BLOCK] * rs[2 * pr]
                o_odd = o4[(2 * pr + 1) * BLOCK:(2 * pr + 2) * BLOCK] * rs[2 * pr + 1]
                p = 2 * g + pr
                attn_s[pl.ds(r0, BLOCK), p * LANES:(p + 1) * LANES] = jnp.where(low_b, o_even, o_odd).astype(BF16)
        return carry

    @pl.when(bounded_s[0] == 1)
    def _attention_constant_shift():
        lax.fori_loop(0, n_blocks, functools.partial(_attn_block, True), 0)

    @pl.when(bounded_s[0] != 1)
    def _attention_row_max():
        lax.fori_loop(0, n_blocks, functools.partial(_attn_block, False), 0)

    gate_a = jnp.concatenate([ga_s[i] for i in range(N_COL_CHUNKS)], axis=1).astype(F32)
    merged = gate_a * _dot(attn_s[...], w_ao_ref[...])

    cv = jnp.concatenate([conv_s[c] for c in range(N_LANE_BLOCKS)], axis=1)
    mu = jnp.mean(cv, axis=-1, keepdims=True)
    xc = cv - mu
    var = jnp.mean(xc * xc, axis=-1, keepdims=True)
    y = xc * lax.rsqrt(var + EPS) * ln_g_ref[...] + ln_b_ref[...]
    conv = _dot((y * jax.nn.sigmoid(y)).astype(BF16), w_co_ref[...])

    gate_c = jnp.concatenate([gc_s[i] for i in range(N_COL_CHUNKS)], axis=1).astype(F32)
    merged = merged + gate_c * conv
    o_ref[0] = x + _dot(merged.astype(BF16), w_out_ref[...])


def _mlp_kernel(x_ref, g_ref, w1_ref, w2_ref, o_ref, xn_s):
    x = x_ref[...]
    xn_s[...] = _rms(x, g_ref[...]).astype(BF16)
    acc = x
    for c in range(D_FF // FF_CHUNK):
        hmid = jnp.square(jnp.maximum(_dot(xn_s[...], w1_ref[c]), 0.0))
        acc = acc + _dot(hmid.astype(BF16), w2_ref[c])
    o_ref[...] = acc


def _resident(shape):
    zeros = (0,) * len(shape)
    return pl.BlockSpec(shape, lambda *_: zeros, pipeline_mode=pl.Buffered(1))


def _smem():
    return pl.BlockSpec(memory_space=pltpu.SMEM)


def kernel(x, norm_mix_g, w_in, q_norm_g, k_norm_g, attn_sinks, rel_bias, w_attn_o, w_dw, b_dw,
           conv_ln_g, conv_ln_b, w_conv_out, w_out, norm_mlp_g, w_ff1, w_ff2):
    batch, seq, d = x.shape
    assert d == D_MODEL and seq % MIX_TILE == 0 and (batch * seq) % MLP_TILE == 0
    layer = 0
    row = lambda v: v.reshape(1, -1).astype(F32)
    w = w_in[layer].astype(BF16)
    q_end = N_Q_HEADS * HEAD_DIM
    k_end = q_end + KV_WIDTH
    v_end = k_end + KV_WIDTH
    a_end = v_end + D_MODEL
    g_end = a_end + D_MODEL
    ga_end = g_end + D_MODEL
    lane_blocks = lambda v: v.astype(F32).reshape(-1, N_LANE_BLOCKS, LANES).transpose(1, 0, 2)
    col_chunks = lambda m: m.reshape(m.shape[0], N_COL_CHUNKS, COL_CHUNK).transpose(1, 0, 2)
    q_gain = jnp.tile(q_norm_g[layer].astype(F32), N_Q_HEADS) * (HEAD_DIM ** -0.5 * LOG2E)
    qk_bound = (QK_BOUND_MARGIN * HEAD_DIM * jnp.max(jnp.abs(q_gain))
                * jnp.max(jnp.abs(k_norm_g[layer].astype(F32))))

    operands = [
        x,
        jnp.asarray(_bucket_table()),
        rel_bias.astype(F32),
        attn_sinks[layer].astype(F32),
        qk_bound.reshape(1),
        jnp.asarray(_head_mean_matrix(), dtype=BF16),
        row(norm_mix_g[layer]),
        q_gain.reshape(N_COL_CHUNKS, 1, COL_CHUNK),
        row(jnp.tile(k_norm_g[layer], N_KV_HEADS)),
        col_chunks(w[:, :q_end]), w[:, q_end:v_end], w[:, v_end:a_end], w[:, a_end:g_end],
        col_chunks(w[:, g_end:ga_end]), col_chunks(w[:, ga_end:]),
        w_attn_o[layer].astype(BF16),
        lane_blocks(w_dw[layer]),
        lane_blocks(b_dw[layer].reshape(1, -1)),
        row(conv_ln_g[layer]), row(conv_ln_b[layer]),
        w_conv_out[layer].astype(BF16),
        w_out[layer].astype(BF16),
    ]
    in_specs = [pl.BlockSpec((1, MIX_TILE, D_MODEL), lambda b, j: (b, j, 0)),
                _resident(operands[1].shape), _smem(), _smem(), _smem()]
    in_specs += [_resident(op.shape) for op in operands[5:]]

    x1 = pl.pallas_call(
        _mixer_kernel,
        grid=(batch, seq // MIX_TILE),
        in_specs=in_specs,
        out_specs=pl.BlockSpec((1, MIX_TILE, D_MODEL), lambda b, j: (b, j, 0)),
        out_shape=jax.ShapeDtypeStruct(x.shape, F32),
        scratch_shapes=[
            pltpu.VMEM((2, N_Q_HEADS, BLOCK, 2 * BLOCK), F32),
            pltpu.SMEM((N_Q_HEADS,), F32),
            pltpu.SMEM((1,), jnp.int32),
            pltpu.VMEM((MIX_TILE, D_MODEL), BF16),
            pltpu.VMEM((2, N_COL_CHUNKS, MIX_TILE, COL_CHUNK), BF16),
            pltpu.VMEM((N_KV_HEADS, BLOCK + MIX_TILE, LANES), BF16),
            pltpu.VMEM((N_KV_HEADS, BLOCK + MIX_TILE, LANES), BF16),
            pltpu.VMEM((MIX_TILE, D_MODEL), BF16),
            pltpu.VMEM((N_LANE_BLOCKS, CONV_HALO + MIX_TILE, LANES), F32),
            pltpu.VMEM((N_LANE_BLOCKS, MIX_TILE, LANES), F32),
            pltpu.VMEM((N_COL_CHUNKS, MIX_TILE, COL_CHUNK), BF16),
            pltpu.VMEM((N_COL_CHUNKS, MIX_TILE, COL_CHUNK), BF16),
        ],
        compiler_params=pltpu.CompilerParams(
            dimension_semantics=("arbitrary", "arbitrary"), vmem_limit_bytes=VMEM_LIMIT),
        name="mixer",
    )(*operands)

    tokens = batch * seq
    out = pl.pallas_call(
        _mlp_kernel,
        grid=(tokens // MLP_TILE,),
        in_specs=[pl.BlockSpec((MLP_TILE, D_MODEL), lambda i: (i, 0)),
                  _resident((1, D_MODEL)), _resident((D_FF // FF_CHUNK, D_MODEL, FF_CHUNK)),
                  _resident((D_FF // FF_CHUNK, FF_CHUNK, D_MODEL))],
        out_specs=pl.BlockSpec((MLP_TILE, D_MODEL), lambda i: (i, 0)),
        out_shape=jax.ShapeDtypeStruct((tokens, D_MODEL), F32),
        scratch_shapes=[pltpu.VMEM((MLP_TILE, D_MODEL), BF16)],
        compiler_params=pltpu.CompilerParams(
            dimension_semantics=("arbitrary",), vmem_limit_bytes=VMEM_LIMIT),
        name="mlp",
    )(x1.reshape(tokens, D_MODEL), row(norm_mlp_g[layer]),
      w_ff1[layer].astype(BF16).reshape(D_MODEL, D_FF // FF_CHUNK, FF_CHUNK).transpose(1, 0, 2),
      w_ff2[layer].astype(BF16).reshape(D_FF // FF_CHUNK, FF_CHUNK, D_MODEL))
    return out.reshape(batch, seq, D_MODEL)
```

```python
import functools

import numpy as np
import jax
import jax.numpy as jnp
from jax import lax
from jax.experimental import pallas as pl
from jax.experimental.pallas import tpu as pltpu

D_MODEL = 1024
HEAD_DIM = 64
N_Q_HEADS = 16
N_KV_HEADS = 4
KV_WIDTH = N_KV_HEADS * HEAD_DIM
GROUP = N_Q_HEADS // N_KV_HEADS
BLOCK = 128
CONV_WIDTH = 31
D_FF = 4 * D_MODEL
N_BUCKETS = 32
MAX_DISTANCE = 128
EPS = 1e-6
NEG = -1e30
LOG2E = 1.4426950408889634
QK_BOUND_MARGIN = 1.03
MAX_EXP2_SPREAD = 100.0

LANES = 128
SUBLANES = 8
N_LANE_BLOCKS = D_MODEL // LANES
COL_CHUNK = 2 * LANES
N_COL_CHUNKS = D_MODEL // COL_CHUNK
CONV_HALO = 32
CONV_SKIP = CONV_HALO - (CONV_WIDTH - 1)
CONV_ROWS = 64

MIX_TILE = 512
MLP_TILE = 512
FF_CHUNK = 1024
VMEM_LIMIT = 56 * 1024 * 1024

F32 = jnp.float32
BF16 = jnp.bfloat16


def _bucket_table():
    qi = np.arange(BLOCK, dtype=np.int32)[:, None]
    kj = np.arange(2 * BLOCK, dtype=np.int32)[None, :]
    dist = qi + BLOCK - kj
    n = np.maximum(dist, 0)
    max_exact = N_BUCKETS // 2
    nf = np.maximum(n, 1).astype(np.float32)
    large = max_exact + (np.log(nf / np.float32(max_exact)) / np.float32(np.log(MAX_DISTANCE / max_exact))
                         * np.float32(N_BUCKETS - max_exact)).astype(np.int32)
    large = np.minimum(large, N_BUCKETS - 1)
    bucket = np.where(n < max_exact, n, large)
    valid = (dist >= 0) & (dist < BLOCK)
    return np.where(valid, bucket, -1).astype(np.int32)


def _head_mean_matrix():
    idx = np.arange(COL_CHUNK) // HEAD_DIM
    return (idx[:, None] == idx[None, :]).astype(np.float32) / HEAD_DIM


def _dot(a, b):
    return jnp.dot(a, b, preferred_element_type=F32)


def _dot_nt(a, b):
    return lax.dot_general(a, b, (((1,), (1,)), ((), ())), preferred_element_type=F32)


def _rms(x, gain):
    return x * lax.rsqrt(jnp.mean(x * x, axis=-1, keepdims=True) + EPS) * gain


def _head_rms(t, hm, gain):
    ms = _dot((t * t).astype(BF16), hm)
    return t * lax.rsqrt(ms + EPS) * gain


def _mixer_kernel(x_ref, bucket_ref, relb_ref, sink_ref, qk_bound_ref, hm_ref,
                  g_mix_ref, gq_ref, gk_ref, w_q_ref, w_kv_ref, w_a_ref, w_g_ref,
                  w_ga_ref, w_gc_ref, w_ao_ref, w_dw_ref, b_dw_ref, ln_g_ref, ln_b_ref,
                  w_co_ref, w_out_ref,
                  o_ref,
                  bias_s, sink_s, bounded_s, u_s, q_s, kdup_s, vdup_s, attn_s, h_s, conv_s, ga_s, gc_s):
    tile = x_ref.shape[1]
    n_blocks = tile // BLOCK
    b = pl.program_id(0)
    j = pl.program_id(1)

    @pl.when((b == 0) & (j == 0))
    def _build_bias():
        qk_bound = qk_bound_ref[0]
        his = [relb_ref[0, h] for h in range(N_Q_HEADS)]
        los = list(his)
        for bb in range(1, N_BUCKETS):
            his = [jnp.maximum(hi, relb_ref[bb, h]) for h, hi in enumerate(his)]
            los = [jnp.minimum(lo, relb_ref[bb, h]) for h, lo in enumerate(los)]
        spread = (his[0] - los[0]) * LOG2E
        for h in range(1, N_Q_HEADS):
            spread = jnp.maximum(spread, (his[h] - los[h]) * LOG2E)
        bounded = 2.0 * qk_bound + spread <= MAX_EXP2_SPREAD
        bounded_s[0] = bounded.astype(jnp.int32)

        bk = bucket_ref[...]
        left = lax.broadcasted_iota(jnp.int32, bk.shape, 1) < BLOCK
        for h in range(N_Q_HEADS):
            shift = jnp.where(bounded, qk_bound + his[h] * LOG2E, 0.0)
            acc = jnp.full(bk.shape, NEG, F32)
            for bb in range(N_BUCKETS):
                acc = jnp.where(bk == bb, relb_ref[bb, h] * LOG2E - shift, acc)
            bias_s[0, h] = acc
            bias_s[1, h] = jnp.where(left, NEG, acc)
            sink_s[h] = jnp.max(jnp.exp2(jnp.full((SUBLANES, LANES), sink_ref[h] * LOG2E - shift, F32)))

    @pl.when(j == 0)
    def _zero_halo():
        kdup_s[:, 0:BLOCK, :] = jnp.zeros((N_KV_HEADS, BLOCK, LANES), BF16)
        vdup_s[:, 0:BLOCK, :] = jnp.zeros((N_KV_HEADS, BLOCK, LANES), BF16)
        h_s[:, 0:CONV_HALO, :] = jnp.zeros((N_LANE_BLOCKS, CONV_HALO, LANES), F32)

    @pl.when(j != 0)
    def _carry_halo():
        kdup_s[:, 0:BLOCK, :] = kdup_s[:, tile:tile + BLOCK, :]
        vdup_s[:, 0:BLOCK, :] = vdup_s[:, tile:tile + BLOCK, :]
        h_s[:, 0:CONV_HALO, :] = h_s[:, tile:tile + CONV_HALO, :]

    x = x_ref[0]
    u = _rms(x, g_mix_ref[...]).astype(BF16)
    hm = hm_ref[...]

    hglu = _dot(u, w_a_ref[...]) * jax.nn.sigmoid(_dot(u, w_g_ref[...]))
    for c in range(N_LANE_BLOCKS):
        h_s[c, CONV_HALO:CONV_HALO + tile, :] = hglu[:, c * LANES:(c + 1) * LANES]

    kv = _dot(u, w_kv_ref[...])
    kn = _head_rms(kv[:, :KV_WIDTH], hm, gk_ref[...])
    vv = kv[:, KV_WIDTH:]
    low = lax.broadcasted_iota(jnp.int32, (tile, LANES), 1) < HEAD_DIM
    for src, dst in ((kn, kdup_s), (vv, vdup_s)):
        for c in range(KV_WIDTH // LANES):
            s0 = src[:, c * LANES:(c + 1) * LANES]
            s1 = pltpu.roll(s0, HEAD_DIM, 1)
            dst[2 * c + 0, BLOCK:BLOCK + tile, :] = jnp.where(low, s0, s1).astype(BF16)
            dst[2 * c + 1, BLOCK:BLOCK + tile, :] = jnp.where(low, s1, s0).astype(BF16)
    u_s[...] = u

    def _conv_lane_block(c):
        for r in range(tile // CONV_ROWS):
            acc = jnp.broadcast_to(b_dw_ref[c], (CONV_ROWS, LANES))
            for a in range(SUBLANES):
                taps = range(a, CONV_WIDTH, SUBLANES)
                start = r * CONV_ROWS + CONV_SKIP + a
                rows = h_s[c, start:start + CONV_ROWS + SUBLANES * (len(taps) - 1), :]
                for k, t in enumerate(taps):
                    acc = acc + rows[SUBLANES * k:SUBLANES * k + CONV_ROWS] * w_dw_ref[c, t:t + 1, :]
            conv_s[c, r * CONV_ROWS:(r + 1) * CONV_ROWS, :] = acc

    def _conv_and_q(i, carry):
        _conv_lane_block(2 * i)
        _conv_lane_block(2 * i + 1)
        t = _dot(u_s[...], w_q_ref[i])
        for half in range(2):
            c = 2 * i + half
            qn = _head_rms(t[:, half * COL_CHUNK:(half + 1) * COL_CHUNK], hm, gq_ref[c])
            even = (lax.broadcasted_iota(jnp.int32, qn.shape, 1) & HEAD_DIM) == 0
            q_s[0, c] = jnp.where(even, qn, 0.0).astype(BF16)
            q_s[1, c] = jnp.where(even, 0.0, qn).astype(BF16)
        return carry

    def _conv_and_gates(i, carry):
        _conv_lane_block(2 * i + N_COL_CHUNKS)
        _conv_lane_block(2 * i + N_COL_CHUNKS + 1)
        ga = jax.nn.sigmoid(_dot(u_s[...], w_ga_ref[i])).astype(BF16)
        gc = jax.nn.sigmoid(_dot(u_s[...], w_gc_ref[i])).astype(BF16)
        for half in range(2):
            ga_s[2 * i + half] = ga[:, half * COL_CHUNK:(half + 1) * COL_CHUNK]
            gc_s[2 * i + half] = gc[:, half * COL_CHUNK:(half + 1) * COL_CHUNK]
        return carry

    lax.fori_loop(0, N_COL_CHUNKS // 2, _conv_and_q, 0)
    lax.fori_loop(0, N_COL_CHUNKS // 2, _conv_and_gates, 0)

    def _attn_block(bounded, n, carry):
        r0 = pl.multiple_of(n * BLOCK, BLOCK)
        first = jnp.where((j == 0) & (n == 0), 1, 0)
        low_b = lax.broadcasted_iota(jnp.int32, (BLOCK, LANES), 1) < HEAD_DIM
        for g in range(N_KV_HEADS):
            q4 = jnp.concatenate(
                [q_s[i, g, pl.ds(r0, BLOCK), pr * LANES:(pr + 1) * LANES] for pr in range(2) for i in range(2)],
                axis=0)
            kb = kdup_s[g, pl.ds(r0, 2 * BLOCK), :]
            vb = vdup_s[g, pl.ds(r0, 2 * BLOCK), :]
            s4 = _dot_nt(q4, kb)
            es, rs = [], []
            for hh in range(GROUP):
                h = GROUP * g + hh
                s = s4[hh * BLOCK:(hh + 1) * BLOCK] + bias_s[first, h]
                if bounded:
                    e = jnp.exp2(s)
                    denom = jnp.sum(e, axis=-1, keepdims=True) + sink_s[h]
                else:
                    m = jnp.max(s, axis=-1, keepdims=True)
                    e = jnp.exp2(s - m)
                    denom = jnp.sum(e, axis=-1, keepdims=True) + jnp.exp2(sink_ref[h] * LOG2E - m)
                es.append(e.astype(BF16))
                rs.append(1.0 / denom)
            o4 = _dot(jnp.concatenate(es, axis=0), vb)
            for pr in range(2):
                o_even = o4[(2 * pr) * BLOCK:(2 * pr + 1) * BLOCK] * rs[2 * pr]
                o_odd = o4[(2 * pr + 1) * BLOCK:(2 * pr + 2) * BLOCK] * rs[2 * pr + 1]
                p = 2 * g + pr
                attn_s[pl.ds(r0, BLOCK), p * LANES:(p + 1) * LANES] = jnp.where(low_b, o_even, o_odd).astype(BF16)
        return carry

    @pl.when(bounded_s[0] == 1)
    def _attention_constant_shift():
        lax.fori_loop(0, n_blocks, functools.partial(_attn_block, True), 0)

    @pl.when(bounded_s[0] != 1)
    def _attention_row_max():
        lax.fori_loop(0, n_blocks, functools.partial(_attn_block, False), 0)

    gate_a = jnp.concatenate([ga_s[i] for i in range(N_COL_CHUNKS)], axis=1).astype(F32)
    merged = gate_a * _dot(attn_s[...], w_ao_ref[...])

    cv = jnp.concatenate([conv_s[c] for c in range(N_LANE_BLOCKS)], axis=1)
    mu = jnp.mean(cv, axis=-1, keepdims=True)
    xc = cv - mu
    var = jnp.mean(xc * xc, axis=-1, keepdims=True)
    y = xc * lax.rsqrt(var + EPS) * ln_g_ref[...] + ln_b_ref[...]
    conv = _dot((y * jax.nn.sigmoid(y)).astype(BF16), w_co_ref[...])

    gate_c = jnp.concatenate([gc_s[i] for i in range(N_COL_CHUNKS)], axis=1).astype(F32)
    merged = merged + gate_c * conv
    o_ref[0] = x + _dot(merged.astype(BF16), w_out_ref[...])


def _mlp_kernel(x_ref, g_ref, w1_ref, w2_ref, o_ref, xn_s):
    x = x_ref[...]
    xn_s[...] = _rms(x, g_ref[...]).astype(BF16)
    acc = x
    for c in range(D_FF // FF_CHUNK):
        hmid = jnp.square(jnp.maximum(_dot(xn_s[...], w1_ref[c]), 0.0))
        acc = acc + _dot(hmid.astype(BF16), w2_ref[c])
    o_ref[...] = acc


def _resident(shape):
    zeros = (0,) * len(shape)
    return pl.BlockSpec(shape, lambda *_: zeros, pipeline_mode=pl.Buffered(1))


def _smem():
    return pl.BlockSpec(memory_space=pltpu.SMEM)


def kernel(x, norm_mix_g, w_in, q_norm_g, k_norm_g, attn_sinks, rel_bias, w_attn_o, w_dw, b_dw,
           conv_ln_g, conv_ln_b, w_conv_out, w_out, norm_mlp_g, w_ff1, w_ff2):
    batch, seq, d = x.shape
    assert d == D_MODEL and seq % MIX_TILE == 0 and (batch * seq) % MLP_TILE == 0
    layer = 0
    row = lambda v: v.reshape(1, -1).astype(F32)
    w = w_in[layer].astype(BF16)
    q_end = N_Q_HEADS * HEAD_DIM
    k_end = q_end + KV_WIDTH
    v_end = k_end + KV_WIDTH
    a_end = v_end + D_MODEL
    g_end = a_end + D_MODEL
    ga_end = g_end + D_MODEL
    lane_blocks = lambda v: v.astype(F32).reshape(-1, N_LANE_BLOCKS, LANES).transpose(1, 0, 2)
    col_chunks = lambda m: m.reshape(m.shape[0], N_COL_CHUNKS // 2, 2 * COL_CHUNK).transpose(1, 0, 2)
    q_gain = jnp.tile(q_norm_g[layer].astype(F32), N_Q_HEADS) * (HEAD_DIM ** -0.5 * LOG2E)
    qk_bound = (QK_BOUND_MARGIN * HEAD_DIM * jnp.max(jnp.abs(q_gain))
                * jnp.max(jnp.abs(k_norm_g[layer].astype(F32))))

    operands = [
        x,
        jnp.asarray(_bucket_table()),
        rel_bias.astype(F32),
        attn_sinks[layer].astype(F32),
        qk_bound.reshape(1),
        jnp.asarray(_head_mean_matrix(), dtype=BF16),
        row(norm_mix_g[layer]),
        q_gain.reshape(N_COL_CHUNKS, 1, COL_CHUNK),
        row(jnp.tile(k_norm_g[layer], N_KV_HEADS)),
        col_chunks(w[:, :q_end]), w[:, q_end:v_end], w[:, v_end:a_end], w[:, a_end:g_end],
        col_chunks(w[:, g_end:ga_end]), col_chunks(w[:, ga_end:]),
        w_attn_o[layer].astype(BF16),
        lane_blocks(w_dw[layer]),
        lane_blocks(b_dw[layer].reshape(1, -1)),
        row(conv_ln_g[layer]), row(conv_ln_b[layer]),
        w_conv_out[layer].astype(BF16),
        w_out[layer].astype(BF16),
    ]
    in_specs = [pl.BlockSpec((1, MIX_TILE, D_MODEL), lambda b, j: (b, j, 0)),
                _resident(operands[1].shape), _smem(), _smem(), _smem()]
    in_specs += [_resident(op.shape) for op in operands[5:]]

    x1 = pl.pallas_call(
        _mixer_kernel,
        grid=(batch, seq // MIX_TILE),
        in_specs=in_specs,
        out_specs=pl.BlockSpec((1, MIX_TILE, D_MODEL), lambda b, j: (b, j, 0)),
        out_shape=jax.ShapeDtypeStruct(x.shape, F32),
        scratch_shapes=[
            pltpu.VMEM((2, N_Q_HEADS, BLOCK, 2 * BLOCK), F32),
            pltpu.SMEM((N_Q_HEADS,), F32),
            pltpu.SMEM((1,), jnp.int32),
            pltpu.VMEM((MIX_TILE, D_MODEL), BF16),
            pltpu.VMEM((2, N_COL_CHUNKS, MIX_TILE, COL_CHUNK), BF16),
            pltpu.VMEM((N_KV_HEADS, BLOCK + MIX_TILE, LANES), BF16),
            pltpu.VMEM((N_KV_HEADS, BLOCK + MIX_TILE, LANES), BF16),
            pltpu.VMEM((MIX_TILE, D_MODEL), BF16),
            pltpu.VMEM((N_LANE_BLOCKS, CONV_HALO + MIX_TILE, LANES), F32),
            pltpu.VMEM((N_LANE_BLOCKS, MIX_TILE, LANES), F32),
            pltpu.VMEM((N_COL_CHUNKS, MIX_TILE, COL_CHUNK), BF16),
            pltpu.VMEM((N_COL_CHUNKS, MIX_TILE, COL_CHUNK), BF16),
        ],
        compiler_params=pltpu.CompilerParams(
            dimension_semantics=("arbitrary", "arbitrary"), vmem_limit_bytes=VMEM_LIMIT),
        name="mixer",
    )(*operands)

    tokens = batch * seq
    out = pl.pallas_call(
        _mlp_kernel,
        grid=(tokens // MLP_TILE,),
        in_specs=[pl.BlockSpec((MLP_TILE, D_MODEL), lambda i: (i, 0)),
                  _resident((1, D_MODEL)), _resident((D_FF // FF_CHUNK, D_MODEL, FF_CHUNK)),
                  _resident((D_FF // FF_CHUNK, FF_CHUNK, D_MODEL))],
        out_specs=pl.BlockSpec((MLP_TILE, D_MODEL), lambda i: (i, 0)),
        out_shape=jax.ShapeDtypeStruct((tokens, D_MODEL), F32),
        scratch_shapes=[pltpu.VMEM((MLP_TILE, D_MODEL), BF16)],
        compiler_params=pltpu.CompilerParams(
            dimension_semantics=("arbitrary",), vmem_limit_bytes=VMEM_LIMIT),
        name="mlp",
    )(x1.reshape(tokens, D_MODEL), row(norm_mlp_g[layer]),
      w_ff1[layer].astype(BF16).reshape(D_MODEL, D_FF // FF_CHUNK, FF_CHUNK).transpose(1, 0, 2),
      w_ff2[layer].astype(BF16).reshape(D_FF // FF_CHUNK, FF_CHUNK, D_MODEL))
    return out.reshape(batch, seq, D_MODEL)
```

```python
import functools

import numpy as np
import jax
import jax.numpy as jnp
from jax import lax
from jax.experimental import pallas as pl
from jax.experimental.pallas import tpu as pltpu

D_MODEL = 1024
HEAD_DIM = 64
N_Q_HEADS = 16
N_KV_HEADS = 4
KV_WIDTH = N_KV_HEADS * HEAD_DIM
GROUP = N_Q_HEADS // N_KV_HEADS
BLOCK = 128
CONV_WIDTH = 31
D_FF = 4 * D_MODEL
N_BUCKETS = 32
MAX_DISTANCE = 128
EPS = 1e-6
NEG = -1e30
LOG2E = 1.4426950408889634
QK_BOUND_MARGIN = 1.03
MAX_EXP2_SPREAD = 100.0

LANES = 128
SUBLANES = 8
N_LANE_BLOCKS = D_MODEL // LANES
COL_CHUNK = 2 * LANES
N_COL_CHUNKS = D_MODEL // COL_CHUNK
CONV_HALO = 32
CONV_SKIP = CONV_HALO - (CONV_WIDTH - 1)
CONV_ROWS = 64

MIX_TILE = 512
MLP_TILE = 512
FF_CHUNK = 1024
VMEM_LIMIT = 56 * 1024 * 1024

F32 = jnp.float32
BF16 = jnp.bfloat16


def _bucket_table():
    qi = np.arange(BLOCK, dtype=np.int32)[:, None]
    kj = np.arange(2 * BLOCK, dtype=np.int32)[None, :]
    dist = qi + BLOCK - kj
    n = np.maximum(dist, 0)
    max_exact = N_BUCKETS // 2
    nf = np.maximum(n, 1).astype(np.float32)
    large = max_exact + (np.log(nf / np.float32(max_exact)) / np.float32(np.log(MAX_DISTANCE / max_exact))
                         * np.float32(N_BUCKETS - max_exact)).astype(np.int32)
    large = np.minimum(large, N_BUCKETS - 1)
    bucket = np.where(n < max_exact, n, large)
    valid = (dist >= 0) & (dist < BLOCK)
    return np.where(valid, bucket, -1).astype(np.int32)


def _head_mean_matrix():
    idx = np.arange(COL_CHUNK) // HEAD_DIM
    return (idx[:, None] == idx[None, :]).astype(np.float32) / HEAD_DIM


def _dot(a, b):
    return jnp.dot(a, b, preferred_element_type=F32)


def _dot_nt(a, b):
    return lax.dot_general(a, b, (((1,), (1,)), ((), ())), preferred_element_type=F32)


def _rms(x, gain):
    return x * lax.rsqrt(jnp.mean(x * x, axis=-1, keepdims=True) + EPS) * gain


def _head_rms(t, hm, gain):
    ms = _dot((t * t).astype(BF16), hm)
    return t * lax.rsqrt(ms + EPS) * gain


def _mixer_kernel(x_ref, bucket_ref, relb_ref, sink_ref, qk_bound_ref, hm_ref,
                  g_mix_ref, gq_ref, gk_ref, w_q_ref, w_kv_ref, w_a_ref, w_g_ref,
                  w_ga_ref, w_gc_ref, w_ao_ref, w_dw_ref, b_dw_ref, ln_g_ref, ln_b_ref,
                  w_co_ref, w_out_ref,
                  o_ref,
                  bias_s, sink_s, bounded_s, u_s, q_s, kdup_s, vdup_s, attn_s, h_s, conv_s, ga_s, gc_s):
    tile = x_ref.shape[1]
    n_blocks = tile // BLOCK
    b = pl.program_id(0)
    j = pl.program_id(1)

    @pl.when((b == 0) & (j == 0))
    def _build_bias():
        qk_bound = qk_bound_ref[0]
        his = [relb_ref[0, h] for h in range(N_Q_HEADS)]
        los = list(his)
        for bb in range(1, N_BUCKETS):
            his = [jnp.maximum(hi, relb_ref[bb, h]) for h, hi in enumerate(his)]
            los = [jnp.minimum(lo, relb_ref[bb, h]) for h, lo in enumerate(los)]
        spread = (his[0] - los[0]) * LOG2E
        for h in range(1, N_Q_HEADS):
            spread = jnp.maximum(spread, (his[h] - los[h]) * LOG2E)
        bounded = 2.0 * qk_bound + spread <= MAX_EXP2_SPREAD
        bounded_s[0] = bounded.astype(jnp.int32)

        bk = bucket_ref[...]
        left = lax.broadcasted_iota(jnp.int32, bk.shape, 1) < BLOCK
        for h in range(N_Q_HEADS):
            shift = jnp.where(bounded, qk_bound + his[h] * LOG2E, 0.0)
            acc = jnp.full(bk.shape, NEG, F32)
            for bb in range(N_BUCKETS):
                acc = jnp.where(bk == bb, relb_ref[bb, h] * LOG2E - shift, acc)
            bias_s[0, h] = acc
            bias_s[1, h] = jnp.where(left, NEG, acc)
            sink_s[h] = jnp.max(jnp.exp2(jnp.full((SUBLANES, LANES), sink_ref[h] * LOG2E - shift, F32)))

    @pl.when(j == 0)
    def _zero_halo():
        kdup_s[:, 0:BLOCK, :] = jnp.zeros((N_KV_HEADS, BLOCK, LANES), BF16)
        vdup_s[:, 0:BLOCK, :] = jnp.zeros((N_KV_HEADS, BLOCK, LANES), BF16)
        h_s[:, 0:CONV_HALO, :] = jnp.zeros((N_LANE_BLOCKS, CONV_HALO, LANES), F32)

    @pl.when(j != 0)
    def _carry_halo():
        kdup_s[:, 0:BLOCK, :] = kdup_s[:, tile:tile + BLOCK, :]
        vdup_s[:, 0:BLOCK, :] = vdup_s[:, tile:tile + BLOCK, :]
        h_s[:, 0:CONV_HALO, :] = h_s[:, tile:tile + CONV_HALO, :]

    x = x_ref[0]
    u = _rms(x, g_mix_ref[...]).astype(BF16)
    hm = hm_ref[...]

    hglu = _dot(u, w_a_ref[...]) * jax.nn.sigmoid(_dot(u, w_g_ref[...]))
    for c in range(N_LANE_BLOCKS):
        h_s[c, CONV_HALO:CONV_HALO + tile, :] = hglu[:, c * LANES:(c + 1) * LANES]

    kv = _dot(u, w_kv_ref[...])
    kn = _head_rms(kv[:, :KV_WIDTH], hm, gk_ref[...])
    vv = kv[:, KV_WIDTH:]
    low = lax.broadcasted_iota(jnp.int32, (tile, LANES), 1) < HEAD_DIM
    for src, dst in ((kn, kdup_s), (vv, vdup_s)):
        for c in range(KV_WIDTH // LANES):
            s0 = src[:, c * LANES:(c + 1) * LANES]
            s1 = pltpu.roll(s0, HEAD_DIM, 1)
            dst[2 * c + 0, BLOCK:BLOCK + tile, :] = jnp.where(low, s0, s1).astype(BF16)
            dst[2 * c + 1, BLOCK:BLOCK + tile, :] = jnp.where(low, s1, s0).astype(BF16)
    u_s[...] = u

    def _conv_lane_block(c):
        for r in range(tile // CONV_ROWS):
            acc = jnp.broadcast_to(b_dw_ref[c], (CONV_ROWS, LANES))
            for a in range(SUBLANES):
                taps = range(a, CONV_WIDTH, SUBLANES)
                start = r * CONV_ROWS + CONV_SKIP + a
                rows = h_s[c, start:start + CONV_ROWS + SUBLANES * (len(taps) - 1), :]
                for k, t in enumerate(taps):
                    acc = acc + rows[SUBLANES * k:SUBLANES * k + CONV_ROWS] * w_dw_ref[c, t:t + 1, :]
            conv_s[c, r * CONV_ROWS:(r + 1) * CONV_ROWS, :] = acc

    def _conv_and_q(i, carry):
        _conv_lane_block(i)
        qn = _head_rms(_dot(u_s[...], w_q_ref[i]), hm, gq_ref[i])
        even = (lax.broadcasted_iota(jnp.int32, qn.shape, 1) & HEAD_DIM) == 0
        q_s[0, i] = jnp.where(even, qn, 0.0).astype(BF16)
        q_s[1, i] = jnp.where(even, 0.0, qn).astype(BF16)
        return carry

    def _conv_and_gates(i, carry):
        _conv_lane_block(i + N_COL_CHUNKS)
        ga_s[i] = _dot(u_s[...], w_ga_ref[i]).astype(BF16)
        gc_s[i] = _dot(u_s[...], w_gc_ref[i]).astype(BF16)
        return carry

    lax.fori_loop(0, N_COL_CHUNKS, _conv_and_q, 0)
    lax.fori_loop(0, N_COL_CHUNKS, _conv_and_gates, 0)

    def _attn_block(bounded, n, carry):
        r0 = pl.multiple_of(n * BLOCK, BLOCK)
        first = jnp.where((j == 0) & (n == 0), 1, 0)
        low_b = lax.broadcasted_iota(jnp.int32, (BLOCK, LANES), 1) < HEAD_DIM
        for g in range(N_KV_HEADS):
            q4 = jnp.concatenate(
                [q_s[i, g, pl.ds(r0, BLOCK), pr * LANES:(pr + 1) * LANES] for pr in range(2) for i in range(2)],
                axis=0)
            kb = kdup_s[g, pl.ds(r0, 2 * BLOCK), :]
            vb = vdup_s[g, pl.ds(r0, 2 * BLOCK), :]
            s4 = _dot_nt(q4, kb)
            es, rs = [], []
            for hh in range(GROUP):
                h = GROUP * g + hh
                s = s4[hh * BLOCK:(hh + 1) * BLOCK] + bias_s[first, h]
                if bounded:
                    e = jnp.exp2(s)
                    denom = jnp.sum(e, axis=-1, keepdims=True) + sink_s[h]
                else:
                    m = jnp.max(s, axis=-1, keepdims=True)
                    e = jnp.exp2(s - m)
                    denom = jnp.sum(e, axis=-1, keepdims=True) + jnp.exp2(sink_ref[h] * LOG2E - m)
                es.append(e.astype(BF16))
                rs.append(1.0 / denom)
            o4 = _dot(jnp.concatenate(es, axis=0), vb)
            for pr in range(2):
                o_even = o4[(2 * pr) * BLOCK:(2 * pr + 1) * BLOCK] * rs[2 * pr]
                o_odd = o4[(2 * pr + 1) * BLOCK:(2 * pr + 2) * BLOCK] * rs[2 * pr + 1]
                p = 2 * g + pr
                attn_s[pl.ds(r0, BLOCK), p * LANES:(p + 1) * LANES] = jnp.where(low_b, o_even, o_odd).astype(BF16)
        return carry

    @pl.when(bounded_s[0] == 1)
    def _attention_constant_shift():
        lax.fori_loop(0, n_blocks, functools.partial(_attn_block, True), 0)

    @pl.when(bounded_s[0] != 1)
    def _attention_row_max():
        lax.fori_loop(0, n_blocks, functools.partial(_attn_block, False), 0)

    gate_a = jax.nn.sigmoid(jnp.concatenate([ga_s[i] for i in range(N_COL_CHUNKS)], axis=1).astype(F32))
    merged = gate_a * _dot(attn_s[...], w_ao_ref[...])

    cv = jnp.concatenate([conv_s[c] for c in range(N_LANE_BLOCKS)], axis=1)
    mu = jnp.mean(cv, axis=-1, keepdims=True)
    xc = cv - mu
    var = jnp.mean(xc * xc, axis=-1, keepdims=True)
    y = xc * lax.rsqrt(var + EPS) * ln_g_ref[...] + ln_b_ref[...]
    conv = _dot((y * jax.nn.sigmoid(y)).astype(BF16), w_co_ref[...])

    gate_c = jax.nn.sigmoid(jnp.concatenate([gc_s[i] for i in range(N_COL_CHUNKS)], axis=1).astype(F32))
    merged = merged + gate_c * conv
    o_ref[0] = x + _dot(merged.astype(BF16), w_out_ref[...])


def _mlp_kernel(x_ref, g_ref, w1_ref, w2_ref, o_ref, xn_s):
    x = x_ref[...]
    xn_s[...] = _rms(x, g_ref[...]).astype(BF16)
    acc = x
    for c in range(D_FF // FF_CHUNK):
        hmid = jnp.square(jnp.maximum(_dot(xn_s[...], w1_ref[c]), 0.0))
        acc = acc + _dot(hmid.astype(BF16), w2_ref[c])
    o_ref[...] = acc


def _resident(shape):
    zeros = (0,) * len(shape)
    return pl.BlockSpec(shape, lambda *_: zeros, pipeline_mode=pl.Buffered(1))


def _smem():
    return pl.BlockSpec(memory_space=pltpu.SMEM)


def kernel(x, norm_mix_g, w_in, q_norm_g, k_norm_g, attn_sinks, rel_bias, w_attn_o, w_dw, b_dw,
           conv_ln_g, conv_ln_b, w_conv_out, w_out, norm_mlp_g, w_ff1, w_ff2):
    batch, seq, d = x.shape
    assert d == D_MODEL and seq % MIX_TILE == 0 and (batch * seq) % MLP_TILE == 0
    layer = 0
    row = lambda v: v.reshape(1, -1).astype(F32)
    w = w_in[layer].astype(BF16)
    q_end = N_Q_HEADS * HEAD_DIM
    k_end = q_end + KV_WIDTH
    v_end = k_end + KV_WIDTH
    a_end = v_end + D_MODEL
    g_end = a_end + D_MODEL
    ga_end = g_end + D_MODEL
    lane_blocks = lambda v: v.astype(F32).reshape(-1, N_LANE_BLOCKS, LANES).transpose(1, 0, 2)
    col_chunks = lambda m: m.reshape(m.shape[0], N_COL_CHUNKS, COL_CHUNK).transpose(1, 0, 2)
    q_gain = jnp.tile(q_norm_g[layer].astype(F32), N_Q_HEADS) * (HEAD_DIM ** -0.5 * LOG2E)
    qk_bound = (QK_BOUND_MARGIN * HEAD_DIM * jnp.max(jnp.abs(q_gain))
                * jnp.max(jnp.abs(k_norm_g[layer].astype(F32))))

    operands = [
        x,
        jnp.asarray(_bucket_table()),
        rel_bias.astype(F32),
        attn_sinks[layer].astype(F32),
        qk_bound.reshape(1),
        jnp.asarray(_head_mean_matrix(), dtype=BF16),
        row(norm_mix_g[layer]),
        q_gain.reshape(N_COL_CHUNKS, 1, COL_CHUNK),
        row(jnp.tile(k_norm_g[layer], N_KV_HEADS)),
        col_chunks(w[:, :q_end]), w[:, q_end:v_end], w[:, v_end:a_end], w[:, a_end:g_end],
        col_chunks(w[:, g_end:ga_end]), col_chunks(w[:, ga_end:]),
        w_attn_o[layer].astype(BF16),
        lane_blocks(w_dw[layer]),
        lane_blocks(b_dw[layer].reshape(1, -1)),
        row(conv_ln_g[layer]), row(conv_ln_b[layer]),
        w_conv_out[layer].astype(BF16),
        w_out[layer].astype(BF16),
    ]
    in_specs = [pl.BlockSpec((1, MIX_TILE, D_MODEL), lambda b, j: (b, j, 0)),
                _resident(operands[1].shape), _smem(), _smem(), _smem()]
    in_specs += [_resident(op.shape) for op in operands[5:]]

    x1 = pl.pallas_call(
        _mixer_kernel,
        grid=(batch, seq // MIX_TILE),
        in_specs=in_specs,
        out_specs=pl.BlockSpec((1, MIX_TILE, D_MODEL), lambda b, j: (b, j, 0)),
        out_shape=jax.ShapeDtypeStruct(x.shape, F32),
        scratch_shapes=[
            pltpu.VMEM((2, N_Q_HEADS, BLOCK, 2 * BLOCK), F32),
            pltpu.SMEM((N_Q_HEADS,), F32),
            pltpu.SMEM((1,), jnp.int32),
            pltpu.VMEM((MIX_TILE, D_MODEL), BF16),
            pltpu.VMEM((2, N_COL_CHUNKS, MIX_TILE, COL_CHUNK), BF16),
            pltpu.VMEM((N_KV_HEADS, BLOCK + MIX_TILE, LANES), BF16),
            pltpu.VMEM((N_KV_HEADS, BLOCK + MIX_TILE, LANES), BF16),
            pltpu.VMEM((MIX_TILE, D_MODEL), BF16),
            pltpu.VMEM((N_LANE_BLOCKS, CONV_HALO + MIX_TILE, LANES), F32),
            pltpu.VMEM((N_LANE_BLOCKS, MIX_TILE, LANES), F32),
            pltpu.VMEM((N_COL_CHUNKS, MIX_TILE, COL_CHUNK), BF16),
            pltpu.VMEM((N_COL_CHUNKS, MIX_TILE, COL_CHUNK), BF16),
        ],
        compiler_params=pltpu.CompilerParams(
            dimension_semantics=("arbitrary", "arbitrary"), vmem_limit_bytes=VMEM_LIMIT),
        name="mixer",
    )(*operands)

    tokens = batch * seq
    out = pl.pallas_call(
        _mlp_kernel,
        grid=(tokens // MLP_TILE,),
        in_specs=[pl.BlockSpec((MLP_TILE, D_MODEL), lambda i: (i, 0)),
                  _resident((1, D_MODEL)), _resident((D_FF // FF_CHUNK, D_MODEL, FF_CHUNK)),
                  _resident((D_FF // FF_CHUNK, FF_CHUNK, D_MODEL))],
        out_specs=pl.BlockSpec((MLP_TILE, D_MODEL), lambda i: (i, 0)),
        out_shape=jax.ShapeDtypeStruct((tokens, D_MODEL), F32),
        scratch_shapes=[pltpu.VMEM((MLP_TILE, D_MODEL), BF16)],
        compiler_params=pltpu.CompilerParams(
            dimension_semantics=("arbitrary",), vmem_limit_bytes=VMEM_LIMIT),
        name="mlp",
    )(x1.reshape(tokens, D_MODEL), row(norm_mlp_g[layer]),
      w_ff1[layer].astype(BF16).reshape(D_MODEL, D_FF // FF_CHUNK, FF_CHUNK).transpose(1, 0, 2),
      w_ff2[layer].astype(BF16).reshape(D_FF // FF_CHUNK, FF_CHUNK, D_MODEL))
    return out.reshape(batch, seq, D_MODEL)
```

```python
import functools

import numpy as np
import jax
import jax.numpy as jnp
from jax import lax
from jax.experimental import pallas as pl
from jax.experimental.pallas import tpu as pltpu

D_MODEL = 1024
HEAD_DIM = 64
N_Q_HEADS = 16
N_KV_HEADS = 4
KV_WIDTH = N_KV_HEADS * HEAD_DIM
GROUP = N_Q_HEADS // N_KV_HEADS
BLOCK = 128
CONV_WIDTH = 31
D_FF = 4 * D_MODEL
N_BUCKETS = 32
MAX_DISTANCE = 128
EPS = 1e-6
NEG = -1e30
LOG2E = 1.4426950408889634
QK_BOUND_MARGIN = 1.03
MAX_EXP2_SPREAD = 100.0

LANES = 128
SUBLANES = 8
N_LANE_BLOCKS = D_MODEL // LANES
COL_CHUNK = 2 * LANES
N_COL_CHUNKS = D_MODEL // COL_CHUNK
CONV_HALO = 32
CONV_SKIP = CONV_HALO - (CONV_WIDTH - 1)
CONV_ROWS = 64

MIX_TILE = 512
MLP_TILE = 512
FF_CHUNK = 1024
VMEM_LIMIT = 56 * 1024 * 1024

F32 = jnp.float32
BF16 = jnp.bfloat16


def _bucket_table():
    qi = np.arange(BLOCK, dtype=np.int32)[:, None]
    kj = np.arange(2 * BLOCK, dtype=np.int32)[None, :]
    dist = qi + BLOCK - kj
    n = np.maximum(dist, 0)
    max_exact = N_BUCKETS // 2
    nf = np.maximum(n, 1).astype(np.float32)
    large = max_exact + (np.log(nf / np.float32(max_exact)) / np.float32(np.log(MAX_DISTANCE / max_exact))
                         * np.float32(N_BUCKETS - max_exact)).astype(np.int32)
    large = np.minimum(large, N_BUCKETS - 1)
    bucket = np.where(n < max_exact, n, large)
    valid = (dist >= 0) & (dist < BLOCK)
    return np.where(valid, bucket, -1).astype(np.int32)


def _head_mean_matrix():
    idx = np.arange(COL_CHUNK) // HEAD_DIM
    return (idx[:, None] == idx[None, :]).astype(np.float32) / HEAD_DIM


def _dot(a, b):
    return jnp.dot(a, b, preferred_element_type=F32)


def _dot_nt(a, b):
    return lax.dot_general(a, b, (((1,), (1,)), ((), ())), preferred_element_type=F32)


def _rms(x, gain):
    return x * lax.rsqrt(jnp.mean(x * x, axis=-1, keepdims=True) + EPS) * gain


def _head_rms(t, hm, gain):
    ms = _dot((t * t).astype(BF16), hm)
    return t * lax.rsqrt(ms + EPS) * gain


def _mixer_kernel(x_ref, bucket_ref, relb_ref, sink_ref, qk_bound_ref, hm_ref,
                  g_mix_ref, gq_ref, gk_ref, w_q_ref, w_kv_ref, w_a_ref, w_g_ref,
                  w_ga_ref, w_gc_ref, w_ao_ref, w_dw_ref, b_dw_ref, ln_g_ref, ln_b_ref,
                  w_co_ref, w_out_ref,
                  o_ref,
                  bias_s, sink_s, bounded_s, u_s, q_s, kdup_s, vdup_s, attn_s, h_s, conv_s, ga_s, gc_s):
    tile = x_ref.shape[1]
    n_blocks = tile // BLOCK
    b = pl.program_id(0)
    j = pl.program_id(1)

    @pl.when((b == 0) & (j == 0))
    def _build_bias():
        qk_bound = qk_bound_ref[0]
        his = [relb_ref[0, h] for h in range(N_Q_HEADS)]
        los = list(his)
        for bb in range(1, N_BUCKETS):
            his = [jnp.maximum(hi, relb_ref[bb, h]) for h, hi in enumerate(his)]
            los = [jnp.minimum(lo, relb_ref[bb, h]) for h, lo in enumerate(los)]
        spread = (his[0] - los[0]) * LOG2E
        for h in range(1, N_Q_HEADS):
            spread = jnp.maximum(spread, (his[h] - los[h]) * LOG2E)
        bounded = 2.0 * qk_bound + spread <= MAX_EXP2_SPREAD
        bounded_s[0] = bounded.astype(jnp.int32)

        bk = bucket_ref[...]
        left = lax.broadcasted_iota(jnp.int32, bk.shape, 1) < BLOCK
        for h in range(N_Q_HEADS):
            shift = jnp.where(bounded, qk_bound + his[h] * LOG2E, 0.0)
            acc = jnp.full(bk.shape, NEG, F32)
            for bb in range(N_BUCKETS):
                acc = jnp.where(bk == bb, relb_ref[bb, h] * LOG2E - shift, acc)
            bias_s[0, h] = acc
            bias_s[1, h] = jnp.where(left, NEG, acc)
            sink_s[h] = jnp.max(jnp.exp2(jnp.full((SUBLANES, LANES), sink_ref[h] * LOG2E - shift, F32)))

    @pl.when(j == 0)
    def _zero_halo():
        kdup_s[:, 0:BLOCK, :] = jnp.zeros((N_KV_HEADS, BLOCK, LANES), BF16)
        vdup_s[:, 0:BLOCK, :] = jnp.zeros((N_KV_HEADS, BLOCK, LANES), BF16)
        h_s[:, 0:CONV_HALO, :] = jnp.zeros((N_LANE_BLOCKS, CONV_HALO, LANES), F32)

    @pl.when(j != 0)
    def _carry_halo():
        kdup_s[:, 0:BLOCK, :] = kdup_s[:, tile:tile + BLOCK, :]
        vdup_s[:, 0:BLOCK, :] = vdup_s[:, tile:tile + BLOCK, :]
        h_s[:, 0:CONV_HALO, :] = h_s[:, tile:tile + CONV_HALO, :]

    x = x_ref[0]
    u = _rms(x, g_mix_ref[...]).astype(BF16)
    hm = hm_ref[...]

    hglu = _dot(u, w_a_ref[...]) * jax.nn.sigmoid(_dot(u, w_g_ref[...]))
    for c in range(N_LANE_BLOCKS):
        h_s[c, CONV_HALO:CONV_HALO + tile, :] = hglu[:, c * LANES:(c + 1) * LANES]

    kv = _dot(u, w_kv_ref[...])
    kn = _head_rms(kv[:, :KV_WIDTH], hm, gk_ref[...])
    vv = kv[:, KV_WIDTH:]
    low = lax.broadcasted_iota(jnp.int32, (tile, LANES), 1) < HEAD_DIM
    for src, dst in ((kn, kdup_s), (vv, vdup_s)):
        for c in range(KV_WIDTH // LANES):
            s0 = src[:, c * LANES:(c + 1) * LANES]
            s1 = pltpu.roll(s0, HEAD_DIM, 1)
            dst[2 * c + 0, BLOCK:BLOCK + tile, :] = jnp.where(low, s0, s1).astype(BF16)
            dst[2 * c + 1, BLOCK:BLOCK + tile, :] = jnp.where(low, s1, s0).astype(BF16)
    u_s[...] = u

    def _conv_lane_block(c):
        for r in range(tile // CONV_ROWS):
            acc = jnp.broadcast_to(b_dw_ref[c], (CONV_ROWS, LANES))
            for a in range(SUBLANES):
                taps = range(a, CONV_WIDTH, SUBLANES)
                start = r * CONV_ROWS + CONV_SKIP + a
                rows = h_s[c, start:start + CONV_ROWS + SUBLANES * (len(taps) - 1), :]
                for k, t in enumerate(taps):
                    acc = acc + rows[SUBLANES * k:SUBLANES * k + CONV_ROWS] * w_dw_ref[c, t:t + 1, :]
            conv_s[c, r * CONV_ROWS:(r + 1) * CONV_ROWS, :] = acc

    def _conv_and_q(i, carry):
        _conv_lane_block(i)
        qn = _head_rms(_dot(u_s[...], w_q_ref[i]), hm, gq_ref[i])
        even = (lax.broadcasted_iota(jnp.int32, qn.shape, 1) & HEAD_DIM) == 0
        q_s[0, i] = jnp.where(even, qn, 0.0).astype(BF16)
        q_s[1, i] = jnp.where(even, 0.0, qn).astype(BF16)
        return carry

    def _conv_and_gates(i, carry):
        _conv_lane_block(i + N_COL_CHUNKS)
        ga_s[i] = _dot(u_s[...], w_ga_ref[i]).astype(BF16)
        gc_s[i] = _dot(u_s[...], w_gc_ref[i]).astype(BF16)
        return carry

    lax.fori_loop(0, N_COL_CHUNKS, _conv_and_q, 0)
    lax.fori_loop(0, N_COL_CHUNKS, _conv_and_gates, 0)

    def _attn_block(bounded, n, carry):
        r0 = pl.multiple_of(n * BLOCK, BLOCK)
        first = jnp.where((j == 0) & (n == 0), 1, 0)
        low_b = lax.broadcasted_iota(jnp.int32, (BLOCK, LANES), 1) < HEAD_DIM
        for g in range(N_KV_HEADS):
            q4 = jnp.concatenate(
                [q_s[i, g, pl.ds(r0, BLOCK), pr * LANES:(pr + 1) * LANES] for pr in range(2) for i in range(2)],
                axis=0)
            kb = kdup_s[g, pl.ds(r0, 2 * BLOCK), :]
            vb = vdup_s[g, pl.ds(r0, 2 * BLOCK), :]
            s4 = _dot_nt(q4, kb)
            es, rs = [], []
            for hh in range(GROUP):
                h = GROUP * g + hh
                s = s4[hh * BLOCK:(hh + 1) * BLOCK] + bias_s[first, h]
                if bounded:
                    e = jnp.exp2(s)
                    denom = jnp.sum(e, axis=-1, keepdims=True) + sink_s[h]
                else:
                    m = jnp.max(s, axis=-1, keepdims=True)
                    e = jnp.exp2(s - m)
                    denom = jnp.sum(e, axis=-1, keepdims=True) + jnp.exp2(sink_ref[h] * LOG2E - m)
                es.append(e.astype(BF16))
                rs.append(1.0 / denom)
            o4 = _dot(jnp.concatenate(es, axis=0), vb)
            for pr in range(2):
                o_even = o4[(2 * pr) * BLOCK:(2 * pr + 1) * BLOCK] * rs[2 * pr]
                o_odd = o4[(2 * pr + 1) * BLOCK:(2 * pr + 2) * BLOCK] * rs[2 * pr + 1]
                p = 2 * g + pr
                attn_s[pl.ds(r0, BLOCK), p * LANES:(p + 1) * LANES] = jnp.where(low_b, o_even, o_odd).astype(BF16)
        return carry

    @pl.when(bounded_s[0] == 1)
    def _attention_constant_shift():
        lax.fori_loop(0, n_blocks, functools.partial(_attn_block, True), 0, unroll=True)

    @pl.when(bounded_s[0] != 1)
    def _attention_row_max():
        lax.fori_loop(0, n_blocks, functools.partial(_attn_block, False), 0)

    gate_a = jax.nn.sigmoid(jnp.concatenate([ga_s[i] for i in range(N_COL_CHUNKS)], axis=1).astype(F32))
    merged = gate_a * _dot(attn_s[...], w_ao_ref[...])

    cv = jnp.concatenate([conv_s[c] for c in range(N_LANE_BLOCKS)], axis=1)
    mu = jnp.mean(cv, axis=-1, keepdims=True)
    xc = cv - mu
    var = jnp.mean(xc * xc, axis=-1, keepdims=True)
    y = xc * lax.rsqrt(var + EPS) * ln_g_ref[...] + ln_b_ref[...]
    conv = _dot((y * jax.nn.sigmoid(y)).astype(BF16), w_co_ref[...])

    gate_c = jax.nn.sigmoid(jnp.concatenate([gc_s[i] for i in range(N_COL_CHUNKS)], axis=1).astype(F32))
    merged = merged + gate_c * conv
    o_ref[0] = x + _dot(merged.astype(BF16), w_out_ref[...])


def _mlp_kernel(x_ref, g_ref, w1_ref, w2_ref, o_ref, xn_s):
    x = x_ref[...]
    xn_s[...] = _rms(x, g_ref[...]).astype(BF16)
    acc = x
    for c in range(D_FF // FF_CHUNK):
        hmid = jnp.square(jnp.maximum(_dot(xn_s[...], w1_ref[c]), 0.0))
        acc = acc + _dot(hmid.astype(BF16), w2_ref[c])
    o_ref[...] = acc


def _resident(shape):
    zeros = (0,) * len(shape)
    return pl.BlockSpec(shape, lambda *_: zeros, pipeline_mode=pl.Buffered(1))


def _smem():
    return pl.BlockSpec(memory_space=pltpu.SMEM)


def kernel(x, norm_mix_g, w_in, q_norm_g, k_norm_g, attn_sinks, rel_bias, w_attn_o, w_dw, b_dw,
           conv_ln_g, conv_ln_b, w_conv_out, w_out, norm_mlp_g, w_ff1, w_ff2):
    batch, seq, d = x.shape
    assert d == D_MODEL and seq % MIX_TILE == 0 and (batch * seq) % MLP_TILE == 0
    layer = 0
    row = lambda v: v.reshape(1, -1).astype(F32)
    w = w_in[layer].astype(BF16)
    q_end = N_Q_HEADS * HEAD_DIM
    k_end = q_end + KV_WIDTH
    v_end = k_end + KV_WIDTH
    a_end = v_end + D_MODEL
    g_end = a_end + D_MODEL
    ga_end = g_end + D_MODEL
    lane_blocks = lambda v: v.astype(F32).reshape(-1, N_LANE_BLOCKS, LANES).transpose(1, 0, 2)
    col_chunks = lambda m: m.reshape(m.shape[0], N_COL_CHUNKS, COL_CHUNK).transpose(1, 0, 2)
    q_gain = jnp.tile(q_norm_g[layer].astype(F32), N_Q_HEADS) * (HEAD_DIM ** -0.5 * LOG2E)
    qk_bound = (QK_BOUND_MARGIN * HEAD_DIM * jnp.max(jnp.abs(q_gain))
                * jnp.max(jnp.abs(k_norm_g[layer].astype(F32))))

    operands = [
        x,
        jnp.asarray(_bucket_table()),
        rel_bias.astype(F32),
        attn_sinks[layer].astype(F32),
        qk_bound.reshape(1),
        jnp.asarray(_head_mean_matrix(), dtype=BF16),
        row(norm_mix_g[layer]),
        q_gain.reshape(N_COL_CHUNKS, 1, COL_CHUNK),
        row(jnp.tile(k_norm_g[layer], N_KV_HEADS)),
        col_chunks(w[:, :q_end]), w[:, q_end:v_end], w[:, v_end:a_end], w[:, a_end:g_end],
        col_chunks(w[:, g_end:ga_end]), col_chunks(w[:, ga_end:]),
        w_attn_o[layer].astype(BF16),
        lane_blocks(w_dw[layer]),
        lane_blocks(b_dw[layer].reshape(1, -1)),
        row(conv_ln_g[layer]), row(conv_ln_b[layer]),
        w_conv_out[layer].astype(BF16),
        w_out[layer].astype(BF16),
    ]
    in_specs = [pl.BlockSpec((1, MIX_TILE, D_MODEL), lambda b, j: (b, j, 0)),
                _resident(operands[1].shape), _smem(), _smem(), _smem()]
    in_specs += [_resident(op.shape) for op in operands[5:]]

    x1 = pl.pallas_call(
        _mixer_kernel,
        grid=(batch, seq // MIX_TILE),
        in_specs=in_specs,
        out_specs=pl.BlockSpec((1, MIX_TILE, D_MODEL), lambda b, j: (b, j, 0)),
        out_shape=jax.ShapeDtypeStruct(x.shape, F32),
        scratch_shapes=[
            pltpu.VMEM((2, N_Q_HEADS, BLOCK, 2 * BLOCK), F32),
            pltpu.SMEM((N_Q_HEADS,), F32),
            pltpu.SMEM((1,), jnp.int32),
            pltpu.VMEM((MIX_TILE, D_MODEL), BF16),
            pltpu.VMEM((2, N_COL_CHUNKS, MIX_TILE, COL_CHUNK), BF16),
            pltpu.VMEM((N_KV_HEADS, BLOCK + MIX_TILE, LANES), BF16),
            pltpu.VMEM((N_KV_HEADS, BLOCK + MIX_TILE, LANES), BF16),
            pltpu.VMEM((MIX_TILE, D_MODEL), BF16),
            pltpu.VMEM((N_LANE_BLOCKS, CONV_HALO + MIX_TILE, LANES), F32),
            pltpu.VMEM((N_LANE_BLOCKS, MIX_TILE, LANES), F32),
            pltpu.VMEM((N_COL_CHUNKS, MIX_TILE, COL_CHUNK), BF16),
            pltpu.VMEM((N_COL_CHUNKS, MIX_TILE, COL_CHUNK), BF16),
        ],
        compiler_params=pltpu.CompilerParams(
            dimension_semantics=("arbitrary", "arbitrary"), vmem_limit_bytes=VMEM_LIMIT),
        name="mixer",
    )(*operands)

    tokens = batch * seq
    out = pl.pallas_call(
        _mlp_kernel,
        grid=(tokens // MLP_TILE,),
        in_specs=[pl.BlockSpec((MLP_TILE, D_MODEL), lambda i: (i, 0)),
                  _resident((1, D_MODEL)), _resident((D_FF // FF_CHUNK, D_MODEL, FF_CHUNK)),
                  _resident((D_FF // FF_CHUNK, FF_CHUNK, D_MODEL))],
        out_specs=pl.BlockSpec((MLP_TILE, D_MODEL), lambda i: (i, 0)),
        out_shape=jax.ShapeDtypeStruct((tokens, D_MODEL), F32),
        scratch_shapes=[pltpu.VMEM((MLP_TILE, D_MODEL), BF16)],
        compiler_params=pltpu.CompilerParams(
            dimension_semantics=("arbitrary",), vmem_limit_bytes=VMEM_LIMIT),
        name="mlp",
    )(x1.reshape(tokens, D_MODEL), row(norm_mlp_g[layer]),
      w_ff1[layer].astype(BF16).reshape(D_MODEL, D_FF // FF_CHUNK, FF_CHUNK).transpose(1, 0, 2),
      w_ff2[layer].astype(BF16).reshape(D_FF // FF_CHUNK, FF_CHUNK, D_MODEL))
    return out.reshape(batch, seq, D_MODEL)
```

```python
import functools

import numpy as np
import jax
import jax.numpy as jnp
from jax import lax
from jax.experimental import pallas as pl
from jax.experimental.pallas import tpu as pltpu

D_MODEL = 1024
HEAD_DIM = 64
N_Q_HEADS = 16
N_KV_HEADS = 4
KV_WIDTH = N_KV_HEADS * HEAD_DIM
GROUP = N_Q_HEADS // N_KV_HEADS
BLOCK = 128
CONV_WIDTH = 31
D_FF = 4 * D_MODEL
N_BUCKETS = 32
MAX_DISTANCE = 128
EPS = 1e-6
NEG = -1e30
LOG2E = 1.4426950408889634
QK_BOUND_MARGIN = 1.03
MAX_EXP2_SPREAD = 100.0

LANES = 128
SUBLANES = 8
N_LANE_BLOCKS = D_MODEL // LANES
COL_CHUNK = 2 * LANES
N_COL_CHUNKS = D_MODEL // COL_CHUNK
CONV_HALO = 32
CONV_SKIP = CONV_HALO - (CONV_WIDTH - 1)
CONV_ROWS = 64

MIX_TILE = 512
MLP_TILE = 1024
FF_CHUNK = 1024
VMEM_LIMIT = 56 * 1024 * 1024

F32 = jnp.float32
BF16 = jnp.bfloat16


def _bucket_table():
    qi = np.arange(BLOCK, dtype=np.int32)[:, None]
    kj = np.arange(2 * BLOCK, dtype=np.int32)[None, :]
    dist = qi + BLOCK - kj
    n = np.maximum(dist, 0)
    max_exact = N_BUCKETS // 2
    nf = np.maximum(n, 1).astype(np.float32)
    large = max_exact + (np.log(nf / np.float32(max_exact)) / np.float32(np.log(MAX_DISTANCE / max_exact))
                         * np.float32(N_BUCKETS - max_exact)).astype(np.int32)
    large = np.minimum(large, N_BUCKETS - 1)
    bucket = np.where(n < max_exact, n, large)
    valid = (dist >= 0) & (dist < BLOCK)
    return np.where(valid, bucket, -1).astype(np.int32)


def _head_mean_matrix():
    idx = np.arange(COL_CHUNK) // HEAD_DIM
    return (idx[:, None] == idx[None, :]).astype(np.float32) / HEAD_DIM


def _dot(a, b):
    return jnp.dot(a, b, preferred_element_type=F32)


def _dot_nt(a, b):
    return lax.dot_general(a, b, (((1,), (1,)), ((), ())), preferred_element_type=F32)


def _rms(x, gain):
    return x * lax.rsqrt(jnp.mean(x * x, axis=-1, keepdims=True) + EPS) * gain


def _head_rms(t, hm, gain):
    ms = _dot((t * t).astype(BF16), hm)
    return t * lax.rsqrt(ms + EPS) * gain


def _mixer_kernel(x_ref, bucket_ref, relb_ref, sink_ref, qk_bound_ref, hm_ref,
                  g_mix_ref, gq_ref, gk_ref, w_q_ref, w_kv_ref, w_a_ref, w_g_ref,
                  w_gates_ref, w_ao_ref, w_dw_ref, b_dw_ref, ln_g_ref, ln_b_ref,
                  w_co_ref, w_out_ref,
                  o_ref,
                  bias_s, sink_s, bounded_s, u_s, q_s, kdup_s, vdup_s, attn_s, h_s, conv_s, ga_s, gc_s):
    tile = x_ref.shape[1]
    n_blocks = tile // BLOCK
    b = pl.program_id(0)
    j = pl.program_id(1)

    @pl.when((b == 0) & (j == 0))
    def _build_bias():
        qk_bound = qk_bound_ref[0]
        his = [relb_ref[0, h] for h in range(N_Q_HEADS)]
        los = list(his)
        for bb in range(1, N_BUCKETS):
            his = [jnp.maximum(hi, relb_ref[bb, h]) for h, hi in enumerate(his)]
            los = [jnp.minimum(lo, relb_ref[bb, h]) for h, lo in enumerate(los)]
        spread = (his[0] - los[0]) * LOG2E
        for h in range(1, N_Q_HEADS):
            spread = jnp.maximum(spread, (his[h] - los[h]) * LOG2E)
        bounded = 2.0 * qk_bound + spread <= MAX_EXP2_SPREAD
        bounded_s[0] = bounded.astype(jnp.int32)

        bk = bucket_ref[...]
        left = lax.broadcasted_iota(jnp.int32, bk.shape, 1) < BLOCK
        for h in range(N_Q_HEADS):
            shift = jnp.where(bounded, qk_bound + his[h] * LOG2E, 0.0)
            acc = jnp.full(bk.shape, NEG, F32)
            for bb in range(N_BUCKETS):
                acc = jnp.where(bk == bb, relb_ref[bb, h] * LOG2E - shift, acc)
            bias_s[0, h] = acc
            bias_s[1, h] = jnp.where(left, NEG, acc)
            sink_s[h] = jnp.max(jnp.exp2(jnp.full((SUBLANES, LANES), sink_ref[h] * LOG2E - shift, F32)))

    @pl.when(j == 0)
    def _zero_halo():
        kdup_s[:, 0:BLOCK, :] = jnp.zeros((N_KV_HEADS, BLOCK, LANES), BF16)
        vdup_s[:, 0:BLOCK, :] = jnp.zeros((N_KV_HEADS, BLOCK, LANES), BF16)
        h_s[:, 0:CONV_HALO, :] = jnp.zeros((N_LANE_BLOCKS, CONV_HALO, LANES), F32)

    @pl.when(j != 0)
    def _carry_halo():
        kdup_s[:, 0:BLOCK, :] = kdup_s[:, tile:tile + BLOCK, :]
        vdup_s[:, 0:BLOCK, :] = vdup_s[:, tile:tile + BLOCK, :]
        h_s[:, 0:CONV_HALO, :] = h_s[:, tile:tile + CONV_HALO, :]

    x = x_ref[0]
    u = _rms(x, g_mix_ref[...]).astype(BF16)
    hm = hm_ref[...]

    hglu = _dot(u, w_a_ref[...]) * jax.nn.sigmoid(_dot(u, w_g_ref[...]))
    for c in range(N_LANE_BLOCKS):
        h_s[c, CONV_HALO:CONV_HALO + tile, :] = hglu[:, c * LANES:(c + 1) * LANES]

    kv = _dot(u, w_kv_ref[...])
    kn = _head_rms(kv[:, :KV_WIDTH], hm, gk_ref[...])
    vv = kv[:, KV_WIDTH:]
    low = lax.broadcasted_iota(jnp.int32, (tile, LANES), 1) < HEAD_DIM
    for src, dst in ((kn, kdup_s), (vv, vdup_s)):
        for c in range(KV_WIDTH // LANES):
            s0 = src[:, c * LANES:(c + 1) * LANES]
            s1 = pltpu.roll(s0, HEAD_DIM, 1)
            dst[2 * c + 0, BLOCK:BLOCK + tile, :] = jnp.where(low, s0, s1).astype(BF16)
            dst[2 * c + 1, BLOCK:BLOCK + tile, :] = jnp.where(low, s1, s0).astype(BF16)
    u_s[...] = u

    def _conv_lane_block(c):
        for r in range(tile // CONV_ROWS):
            acc = jnp.broadcast_to(b_dw_ref[c], (CONV_ROWS, LANES))
            for a in range(SUBLANES):
                taps = range(a, CONV_WIDTH, SUBLANES)
                start = r * CONV_ROWS + CONV_SKIP + a
                rows = h_s[c, start:start + CONV_ROWS + SUBLANES * (len(taps) - 1), :]
                for k, t in enumerate(taps):
                    acc = acc + rows[SUBLANES * k:SUBLANES * k + CONV_ROWS] * w_dw_ref[c, t:t + 1, :]
            conv_s[c, r * CONV_ROWS:(r + 1) * CONV_ROWS, :] = acc

    def _conv_and_q(i, carry):
        _conv_lane_block(i)
        qn = _head_rms(_dot(u_s[...], w_q_ref[i]), hm, gq_ref[i])
        even = (lax.broadcasted_iota(jnp.int32, qn.shape, 1) & HEAD_DIM) == 0
        q_s[0, i] = jnp.where(even, qn, 0.0).astype(BF16)
        q_s[1, i] = jnp.where(even, 0.0, qn).astype(BF16)
        return carry

    def _conv_and_gates(i, carry):
        _conv_lane_block(i + N_COL_CHUNKS)
        gates = _dot(u_s[...], w_gates_ref[i]).astype(BF16)
        ga_s[i] = gates[:, :COL_CHUNK]
        gc_s[i] = gates[:, COL_CHUNK:]
        return carry

    lax.fori_loop(0, N_COL_CHUNKS, _conv_and_q, 0)
    lax.fori_loop(0, N_COL_CHUNKS, _conv_and_gates, 0)

    def _attn_block(bounded, n, carry):
        r0 = pl.multiple_of(n * BLOCK, BLOCK)
        first = jnp.where((j == 0) & (n == 0), 1, 0)
        low_b = lax.broadcasted_iota(jnp.int32, (BLOCK, LANES), 1) < HEAD_DIM
        for g in range(N_KV_HEADS):
            q4 = jnp.concatenate(
                [q_s[i, g, pl.ds(r0, BLOCK), pr * LANES:(pr + 1) * LANES] for pr in range(2) for i in range(2)],
                axis=0)
            kb = kdup_s[g, pl.ds(r0, 2 * BLOCK), :]
            vb = vdup_s[g, pl.ds(r0, 2 * BLOCK), :]
            s4 = _dot_nt(q4, kb)
            es, rs = [], []
            for hh in range(GROUP):
                h = GROUP * g + hh
                s = s4[hh * BLOCK:(hh + 1) * BLOCK] + bias_s[first, h]
                if bounded:
                    e = jnp.exp2(s)
                    denom = jnp.sum(e, axis=-1, keepdims=True) + sink_s[h]
                else:
                    m = jnp.max(s, axis=-1, keepdims=True)
                    e = jnp.exp2(s - m)
                    denom = jnp.sum(e, axis=-1, keepdims=True) + jnp.exp2(sink_ref[h] * LOG2E - m)
                es.append(e.astype(BF16))
                rs.append(1.0 / denom)
            o4 = _dot(jnp.concatenate(es, axis=0), vb)
            for pr in range(2):
                o_even = o4[(2 * pr) * BLOCK:(2 * pr + 1) * BLOCK] * rs[2 * pr]
                o_odd = o4[(2 * pr + 1) * BLOCK:(2 * pr + 2) * BLOCK] * rs[2 * pr + 1]
                p = 2 * g + pr
                attn_s[pl.ds(r0, BLOCK), p * LANES:(p + 1) * LANES] = jnp.where(low_b, o_even, o_odd).astype(BF16)
        return carry

    @pl.when(bounded_s[0] == 1)
    def _attention_constant_shift():
        lax.fori_loop(0, n_blocks, functools.partial(_attn_block, True), 0, unroll=True)

    @pl.when(bounded_s[0] != 1)
    def _attention_row_max():
        lax.fori_loop(0, n_blocks, functools.partial(_attn_block, False), 0)

    gate_a = jax.nn.sigmoid(jnp.concatenate([ga_s[i] for i in range(N_COL_CHUNKS)], axis=1).astype(F32))
    merged = gate_a * _dot(attn_s[...], w_ao_ref[...])

    cv = jnp.concatenate([conv_s[c] for c in range(N_LANE_BLOCKS)], axis=1)
    mu = jnp.mean(cv, axis=-1, keepdims=True)
    xc = cv - mu
    var = jnp.mean(xc * xc, axis=-1, keepdims=True)
    y = xc * lax.rsqrt(var + EPS) * ln_g_ref[...] + ln_b_ref[...]
    conv = _dot((y * jax.nn.sigmoid(y)).astype(BF16), w_co_ref[...])

    gate_c = jax.nn.sigmoid(jnp.concatenate([gc_s[i] for i in range(N_COL_CHUNKS)], axis=1).astype(F32))
    merged = merged + gate_c * conv
    o_ref[0] = x + _dot(merged.astype(BF16), w_out_ref[...])


def _mlp_kernel(x_ref, g_ref, w1_ref, w2_ref, o_ref, xn_s):
    x = x_ref[...]
    xn_s[...] = _rms(x, g_ref[...]).astype(BF16)
    acc = x
    for c in range(D_FF // FF_CHUNK):
        hmid = jnp.square(jnp.maximum(_dot(xn_s[...], w1_ref[c]), 0.0))
        acc = acc + _dot(hmid.astype(BF16), w2_ref[c])
    o_ref[...] = acc


def _resident(shape):
    zeros = (0,) * len(shape)
    return pl.BlockSpec(shape, lambda *_: zeros, pipeline_mode=pl.Buffered(1))


def _smem():
    return pl.BlockSpec(memory_space=pltpu.SMEM)


def kernel(x, norm_mix_g, w_in, q_norm_g, k_norm_g, attn_sinks, rel_bias, w_attn_o, w_dw, b_dw,
           conv_ln_g, conv_ln_b, w_conv_out, w_out, norm_mlp_g, w_ff1, w_ff2):
    batch, seq, d = x.shape
    assert d == D_MODEL and seq % MIX_TILE == 0 and (batch * seq) % MLP_TILE == 0
    layer = 0
    row = lambda v: v.reshape(1, -1).astype(F32)
    w = w_in[layer].astype(BF16)
    q_end = N_Q_HEADS * HEAD_DIM
    k_end = q_end + KV_WIDTH
    v_end = k_end + KV_WIDTH
    a_end = v_end + D_MODEL
    g_end = a_end + D_MODEL
    ga_end = g_end + D_MODEL
    lane_blocks = lambda v: v.astype(F32).reshape(-1, N_LANE_BLOCKS, LANES).transpose(1, 0, 2)
    col_chunks = lambda m: m.reshape(m.shape[0], N_COL_CHUNKS, COL_CHUNK).transpose(1, 0, 2)
    q_gain = jnp.tile(q_norm_g[layer].astype(F32), N_Q_HEADS) * (HEAD_DIM ** -0.5 * LOG2E)
    qk_bound = (QK_BOUND_MARGIN * HEAD_DIM * jnp.max(jnp.abs(q_gain))
                * jnp.max(jnp.abs(k_norm_g[layer].astype(F32))))

    operands = [
        x,
        jnp.asarray(_bucket_table()),
        rel_bias.astype(F32),
        attn_sinks[layer].astype(F32),
        qk_bound.reshape(1),
        jnp.asarray(_head_mean_matrix(), dtype=BF16),
        row(norm_mix_g[layer]),
        q_gain.reshape(N_COL_CHUNKS, 1, COL_CHUNK),
        row(jnp.tile(k_norm_g[layer], N_KV_HEADS)),
        col_chunks(w[:, :q_end]), w[:, q_end:v_end], w[:, v_end:a_end], w[:, a_end:g_end],
        jnp.concatenate([col_chunks(w[:, g_end:ga_end]), col_chunks(w[:, ga_end:])], axis=2),
        w_attn_o[layer].astype(BF16),
        lane_blocks(w_dw[layer]),
        lane_blocks(b_dw[layer].reshape(1, -1)),
        row(conv_ln_g[layer]), row(conv_ln_b[layer]),
        w_conv_out[layer].astype(BF16),
        w_out[layer].astype(BF16),
    ]
    in_specs = [pl.BlockSpec((1, MIX_TILE, D_MODEL), lambda b, j: (b, j, 0)),
                _resident(operands[1].shape), _smem(), _smem(), _smem()]
    in_specs += [_resident(op.shape) for op in operands[5:]]

    x1 = pl.pallas_call(
        _mixer_kernel,
        grid=(batch, seq // MIX_TILE),
        in_specs=in_specs,
        out_specs=pl.BlockSpec((1, MIX_TILE, D_MODEL), lambda b, j: (b, j, 0)),
        out_shape=jax.ShapeDtypeStruct(x.shape, F32),
        scratch_shapes=[
            pltpu.VMEM((2, N_Q_HEADS, BLOCK, 2 * BLOCK), F32),
            pltpu.SMEM((N_Q_HEADS,), F32),
            pltpu.SMEM((1,), jnp.int32),
            pltpu.VMEM((MIX_TILE, D_MODEL), BF16),
            pltpu.VMEM((2, N_COL_CHUNKS, MIX_TILE, COL_CHUNK), BF16),
            pltpu.VMEM((N_KV_HEADS, BLOCK + MIX_TILE, LANES), BF16),
            pltpu.VMEM((N_KV_HEADS, BLOCK + MIX_TILE, LANES), BF16),
            pltpu.VMEM((MIX_TILE, D_MODEL), BF16),
            pltpu.VMEM((N_LANE_BLOCKS, CONV_HALO + MIX_TILE, LANES), F32),
            pltpu.VMEM((N_LANE_BLOCKS, MIX_TILE, LANES), F32),
            pltpu.VMEM((N_COL_CHUNKS, MIX_TILE, COL_CHUNK), BF16),
            pltpu.VMEM((N_COL_CHUNKS, MIX_TILE, COL_CHUNK), BF16),
        ],
        compiler_params=pltpu.CompilerParams(
            dimension_semantics=("arbitrary", "arbitrary"), vmem_limit_bytes=VMEM_LIMIT),
        name="mixer",
    )(*operands)

    tokens = batch * seq
    out = pl.pallas_call(
        _mlp_kernel,
        grid=(tokens // MLP_TILE,),
        in_specs=[pl.BlockSpec((MLP_TILE, D_MODEL), lambda i: (i, 0)),
                  _resident((1, D_MODEL)), _resident((D_FF // FF_CHUNK, D_MODEL, FF_CHUNK)),
                  _resident((D_FF // FF_CHUNK, FF_CHUNK, D_MODEL))],
        out_specs=pl.BlockSpec((MLP_TILE, D_MODEL), lambda i: (i, 0)),
        out_shape=jax.ShapeDtypeStruct((tokens, D_MODEL), F32),
        scratch_shapes=[pltpu.VMEM((MLP_TILE, D_MODEL), BF16)],
        compiler_params=pltpu.CompilerParams(
            dimension_semantics=("arbitrary",), vmem_limit_bytes=VMEM_LIMIT),
        name="mlp",
    )(x1.reshape(tokens, D_MODEL), row(norm_mlp_g[layer]),
      w_ff1[layer].astype(BF16).reshape(D_MODEL, D_FF // FF_CHUNK, FF_CHUNK).transpose(1, 0, 2),
      w_ff2[layer].astype(BF16).reshape(D_FF // FF_CHUNK, FF_CHUNK, D_MODEL))
    return out.reshape(batch, seq, D_MODEL)
```

```python
import functools

import numpy as np
import jax
import jax.numpy as jnp
from jax import lax
from jax.experimental import pallas as pl
from jax.experimental.pallas import tpu as pltpu

D_MODEL = 1024
HEAD_DIM = 64
N_Q_HEADS = 16
N_KV_HEADS = 4
KV_WIDTH = N_KV_HEADS * HEAD_DIM
GROUP = N_Q_HEADS // N_KV_HEADS
BLOCK = 128
CONV_WIDTH = 31
D_FF = 4 * D_MODEL
N_BUCKETS = 32
MAX_DISTANCE = 128
EPS = 1e-6
NEG = -1e30
LOG2E = 1.4426950408889634
QK_BOUND_MARGIN = 1.03
MAX_EXP2_SPREAD = 100.0

LANES = 128
SUBLANES = 8
N_LANE_BLOCKS = D_MODEL // LANES
COL_CHUNK = 2 * LANES
N_COL_CHUNKS = D_MODEL // COL_CHUNK
CONV_HALO = 32
CONV_SKIP = CONV_HALO - (CONV_WIDTH - 1)
CONV_ROWS = 64

MIX_TILE = 512
MLP_TILE = 1024
FF_CHUNK = 1024
VMEM_LIMIT = 56 * 1024 * 1024

F32 = jnp.float32
BF16 = jnp.bfloat16


def _bucket_table():
    qi = np.arange(BLOCK, dtype=np.int32)[:, None]
    kj = np.arange(2 * BLOCK, dtype=np.int32)[None, :]
    dist = qi + BLOCK - kj
    n = np.maximum(dist, 0)
    max_exact = N_BUCKETS // 2
    nf = np.maximum(n, 1).astype(np.float32)
    large = max_exact + (np.log(nf / np.float32(max_exact)) / np.float32(np.log(MAX_DISTANCE / max_exact))
                         * np.float32(N_BUCKETS - max_exact)).astype(np.int32)
    large = np.minimum(large, N_BUCKETS - 1)
    bucket = np.where(n < max_exact, n, large)
    valid = (dist >= 0) & (dist < BLOCK)
    return np.where(valid, bucket, -1).astype(np.int32)


def _head_mean_matrix():
    idx = np.arange(COL_CHUNK) // HEAD_DIM
    return (idx[:, None] == idx[None, :]).astype(np.float32) / HEAD_DIM


def _dot(a, b):
    return jnp.dot(a, b, preferred_element_type=F32)


def _dot_nt(a, b):
    return lax.dot_general(a, b, (((1,), (1,)), ((), ())), preferred_element_type=F32)


def _rms(x, gain):
    return x * lax.rsqrt(jnp.mean(x * x, axis=-1, keepdims=True) + EPS) * gain


def _head_rms(t, hm, gain):
    ms = _dot((t * t).astype(BF16), hm)
    return t * lax.rsqrt(ms + EPS) * gain


def _mixer_kernel(x_ref, x_next_ref, bucket_ref, relb_ref, sink_ref, qk_bound_ref, hm_ref,
                  g_mix_ref, gq_ref, gk_ref, w_q_ref, w_kv_ref, w_a_ref, w_g_ref,
                  w_gates_ref, w_ao_ref, w_dw_ref, b_dw_ref, ln_g_ref, ln_b_ref,
                  w_co_ref, w_out_ref,
                  o_ref,
                  bias_s, sink_s, bounded_s, u_s, q_s, kdup_s, vdup_s, attn_s, h_s, conv_s, ga_s, gc_s):
    tile = x_ref.shape[1]
    n_blocks = tile // BLOCK
    b = pl.program_id(0)
    j = pl.program_id(1)

    @pl.when((b == 0) & (j == 0))
    def _build_bias():
        qk_bound = qk_bound_ref[0]
        his = [relb_ref[0, h] for h in range(N_Q_HEADS)]
        los = list(his)
        for bb in range(1, N_BUCKETS):
            his = [jnp.maximum(hi, relb_ref[bb, h]) for h, hi in enumerate(his)]
            los = [jnp.minimum(lo, relb_ref[bb, h]) for h, lo in enumerate(los)]
        spread = (his[0] - los[0]) * LOG2E
        for h in range(1, N_Q_HEADS):
            spread = jnp.maximum(spread, (his[h] - los[h]) * LOG2E)
        bounded = 2.0 * qk_bound + spread <= MAX_EXP2_SPREAD
        bounded_s[0] = bounded.astype(jnp.int32)

        bk = bucket_ref[...]
        left = lax.broadcasted_iota(jnp.int32, bk.shape, 1) < BLOCK
        for h in range(N_Q_HEADS):
            shift = jnp.where(bounded, qk_bound + his[h] * LOG2E, 0.0)
            acc = jnp.full(bk.shape, NEG, F32)
            for bb in range(N_BUCKETS):
                acc = jnp.where(bk == bb, relb_ref[bb, h] * LOG2E - shift, acc)
            bias_s[0, h] = acc
            bias_s[1, h] = jnp.where(left, NEG, acc)
            sink_s[h] = jnp.max(jnp.exp2(jnp.full((SUBLANES, LANES), sink_ref[h] * LOG2E - shift, F32)))

    @pl.when(j == 0)
    def _zero_halo():
        kdup_s[:, 0:BLOCK, :] = jnp.zeros((N_KV_HEADS, BLOCK, LANES), BF16)
        vdup_s[:, 0:BLOCK, :] = jnp.zeros((N_KV_HEADS, BLOCK, LANES), BF16)
        h_s[:, 0:CONV_HALO, :] = jnp.zeros((N_LANE_BLOCKS, CONV_HALO, LANES), F32)

    @pl.when(j != 0)
    def _carry_halo():
        kdup_s[:, 0:BLOCK, :] = kdup_s[:, tile:tile + BLOCK, :]
        vdup_s[:, 0:BLOCK, :] = vdup_s[:, tile:tile + BLOCK, :]
        h_s[:, 0:CONV_HALO, :] = h_s[:, tile:tile + CONV_HALO, :]

    n_seq_tiles = pl.num_programs(1)
    slot = (b * n_seq_tiles + j) % 2

    @pl.when((b == 0) & (j == 0))
    def _first_rows():
        u_s[0] = _rms(x_ref[0], g_mix_ref[...]).astype(BF16)

    x = x_ref[0]
    hm = hm_ref[...]

    hglu = _dot(u_s[slot], w_a_ref[...]) * jax.nn.sigmoid(_dot(u_s[slot], w_g_ref[...]))
    for c in range(N_LANE_BLOCKS):
        h_s[c, CONV_HALO:CONV_HALO + tile, :] = hglu[:, c * LANES:(c + 1) * LANES]

    kv = _dot(u_s[slot], w_kv_ref[...])
    kn = _head_rms(kv[:, :KV_WIDTH], hm, gk_ref[...])
    vv = kv[:, KV_WIDTH:]
    low = lax.broadcasted_iota(jnp.int32, (tile, LANES), 1) < HEAD_DIM
    for src, dst in ((kn, kdup_s), (vv, vdup_s)):
        for c in range(KV_WIDTH // LANES):
            s0 = src[:, c * LANES:(c + 1) * LANES]
            s1 = pltpu.roll(s0, HEAD_DIM, 1)
            dst[2 * c + 0, BLOCK:BLOCK + tile, :] = jnp.where(low, s0, s1).astype(BF16)
            dst[2 * c + 1, BLOCK:BLOCK + tile, :] = jnp.where(low, s1, s0).astype(BF16)

    def _conv_lane_block(c):
        for r in range(tile // CONV_ROWS):
            acc = jnp.broadcast_to(b_dw_ref[c], (CONV_ROWS, LANES))
            for a in range(SUBLANES):
                taps = range(a, CONV_WIDTH, SUBLANES)
                start = r * CONV_ROWS + CONV_SKIP + a
                rows = h_s[c, start:start + CONV_ROWS + SUBLANES * (len(taps) - 1), :]
                for k, t in enumerate(taps):
                    acc = acc + rows[SUBLANES * k:SUBLANES * k + CONV_ROWS] * w_dw_ref[c, t:t + 1, :]
            conv_s[c, r * CONV_ROWS:(r + 1) * CONV_ROWS, :] = acc

    def _conv_and_q(i, carry):
        _conv_lane_block(i)
        qn = _head_rms(_dot(u_s[slot], w_q_ref[i]), hm, gq_ref[i])
        even = (lax.broadcasted_iota(jnp.int32, qn.shape, 1) & HEAD_DIM) == 0
        q_s[0, i] = jnp.where(even, qn, 0.0).astype(BF16)
        q_s[1, i] = jnp.where(even, 0.0, qn).astype(BF16)
        return carry

    def _conv_and_gates(i, carry):
        _conv_lane_block(i + N_COL_CHUNKS)
        gates = _dot(u_s[slot], w_gates_ref[i]).astype(BF16)
        ga_s[i] = gates[:, :COL_CHUNK]
        gc_s[i] = gates[:, COL_CHUNK:]
        return carry

    lax.fori_loop(0, N_COL_CHUNKS, _conv_and_q, 0)
    lax.fori_loop(0, N_COL_CHUNKS, _conv_and_gates, 0)

    def _attn_block(bounded, n, carry):
        r0 = pl.multiple_of(n * BLOCK, BLOCK)
        first = jnp.where((j == 0) & (n == 0), 1, 0)
        low_b = lax.broadcasted_iota(jnp.int32, (BLOCK, LANES), 1) < HEAD_DIM
        for g in range(N_KV_HEADS):
            q4 = jnp.concatenate(
                [q_s[i, g, pl.ds(r0, BLOCK), pr * LANES:(pr + 1) * LANES] for pr in range(2) for i in range(2)],
                axis=0)
            kb = kdup_s[g, pl.ds(r0, 2 * BLOCK), :]
            vb = vdup_s[g, pl.ds(r0, 2 * BLOCK), :]
            s4 = _dot_nt(q4, kb)
            es, rs = [], []
            for hh in range(GROUP):
                h = GROUP * g + hh
                s = s4[hh * BLOCK:(hh + 1) * BLOCK] + bias_s[first, h]
                if bounded:
                    e = jnp.exp2(s)
                    denom = jnp.sum(e, axis=-1, keepdims=True) + sink_s[h]
                else:
                    m = jnp.max(s, axis=-1, keepdims=True)
                    e = jnp.exp2(s - m)
                    denom = jnp.sum(e, axis=-1, keepdims=True) + jnp.exp2(sink_ref[h] * LOG2E - m)
                es.append(e.astype(BF16))
                rs.append(1.0 / denom)
            o4 = _dot(jnp.concatenate(es, axis=0), vb)
            for pr in range(2):
                o_even = o4[(2 * pr) * BLOCK:(2 * pr + 1) * BLOCK] * rs[2 * pr]
                o_odd = o4[(2 * pr + 1) * BLOCK:(2 * pr + 2) * BLOCK] * rs[2 * pr + 1]
                p = 2 * g + pr
                attn_s[pl.ds(r0, BLOCK), p * LANES:(p + 1) * LANES] = jnp.where(low_b, o_even, o_odd).astype(BF16)
        return carry

    @pl.when(bounded_s[0] == 1)
    def _attention_constant_shift():
        lax.fori_loop(0, n_blocks, functools.partial(_attn_block, True), 0, unroll=True)

    @pl.when(bounded_s[0] != 1)
    def _attention_row_max():
        lax.fori_loop(0, n_blocks, functools.partial(_attn_block, False), 0)

    gate_a = jax.nn.sigmoid(jnp.concatenate([ga_s[i] for i in range(N_COL_CHUNKS)], axis=1).astype(F32))
    merged = gate_a * _dot(attn_s[...], w_ao_ref[...])

    u_s[1 - slot] = _rms(x_next_ref[0], g_mix_ref[...]).astype(BF16)

    cv = jnp.concatenate([conv_s[c] for c in range(N_LANE_BLOCKS)], axis=1)
    mu = jnp.mean(cv, axis=-1, keepdims=True)
    xc = cv - mu
    var = jnp.mean(xc * xc, axis=-1, keepdims=True)
    y = xc * lax.rsqrt(var + EPS) * ln_g_ref[...] + ln_b_ref[...]
    conv = _dot((y * jax.nn.sigmoid(y)).astype(BF16), w_co_ref[...])

    gate_c = jax.nn.sigmoid(jnp.concatenate([gc_s[i] for i in range(N_COL_CHUNKS)], axis=1).astype(F32))
    merged = merged + gate_c * conv
    o_ref[0] = x + _dot(merged.astype(BF16), w_out_ref[...])


def _mlp_kernel(x_ref, g_ref, w1_ref, w2_ref, o_ref, xn_s):
    x = x_ref[...]
    xn_s[...] = _rms(x, g_ref[...]).astype(BF16)
    acc = x
    for c in range(D_FF // FF_CHUNK):
        hmid = jnp.square(jnp.maximum(_dot(xn_s[...], w1_ref[c]), 0.0))
        acc = acc + _dot(hmid.astype(BF16), w2_ref[c])
    o_ref[...] = acc


def _resident(shape):
    zeros = (0,) * len(shape)
    return pl.BlockSpec(shape, lambda *_: zeros, pipeline_mode=pl.Buffered(1))


def _smem():
    return pl.BlockSpec(memory_space=pltpu.SMEM)


def kernel(x, norm_mix_g, w_in, q_norm_g, k_norm_g, attn_sinks, rel_bias, w_attn_o, w_dw, b_dw,
           conv_ln_g, conv_ln_b, w_conv_out, w_out, norm_mlp_g, w_ff1, w_ff2):
    batch, seq, d = x.shape
    assert d == D_MODEL and seq % MIX_TILE == 0 and (batch * seq) % MLP_TILE == 0
    layer = 0
    row = lambda v: v.reshape(1, -1).astype(F32)
    w = w_in[layer].astype(BF16)
    q_end = N_Q_HEADS * HEAD_DIM
    k_end = q_end + KV_WIDTH
    v_end = k_end + KV_WIDTH
    a_end = v_end + D_MODEL
    g_end = a_end + D_MODEL
    ga_end = g_end + D_MODEL
    lane_blocks = lambda v: v.astype(F32).reshape(-1, N_LANE_BLOCKS, LANES).transpose(1, 0, 2)
    col_chunks = lambda m: m.reshape(m.shape[0], N_COL_CHUNKS, COL_CHUNK).transpose(1, 0, 2)
    q_gain = jnp.tile(q_norm_g[layer].astype(F32), N_Q_HEADS) * (HEAD_DIM ** -0.5 * LOG2E)
    qk_bound = (QK_BOUND_MARGIN * HEAD_DIM * jnp.max(jnp.abs(q_gain))
                * jnp.max(jnp.abs(k_norm_g[layer].astype(F32))))

    operands = [
        x, x,
        jnp.asarray(_bucket_table()),
        rel_bias.astype(F32),
        attn_sinks[layer].astype(F32),
        qk_bound.reshape(1),
        jnp.asarray(_head_mean_matrix(), dtype=BF16),
        row(norm_mix_g[layer]),
        q_gain.reshape(N_COL_CHUNKS, 1, COL_CHUNK),
        row(jnp.tile(k_norm_g[layer], N_KV_HEADS)),
        col_chunks(w[:, :q_end]), w[:, q_end:v_end], w[:, v_end:a_end], w[:, a_end:g_end],
        jnp.concatenate([col_chunks(w[:, g_end:ga_end]), col_chunks(w[:, ga_end:])], axis=2),
        w_attn_o[layer].astype(BF16),
        lane_blocks(w_dw[layer]),
        lane_blocks(b_dw[layer].reshape(1, -1)),
        row(conv_ln_g[layer]), row(conv_ln_b[layer]),
        w_conv_out[layer].astype(BF16),
        w_out[layer].astype(BF16),
    ]
    n_seq_tiles = seq // MIX_TILE
    last_tile = batch * n_seq_tiles - 1

    def next_tile(b, j):
        t = jnp.minimum(b * n_seq_tiles + j + 1, last_tile)
        return (t // n_seq_tiles, t % n_seq_tiles, 0)

    in_specs = [pl.BlockSpec((1, MIX_TILE, D_MODEL), lambda b, j: (b, j, 0)),
                pl.BlockSpec((1, MIX_TILE, D_MODEL), next_tile),
                _resident(operands[2].shape), _smem(), _smem(), _smem()]
    in_specs += [_resident(op.shape) for op in operands[6:]]

    x1 = pl.pallas_call(
        _mixer_kernel,
        grid=(batch, seq // MIX_TILE),
        in_specs=in_specs,
        out_specs=pl.BlockSpec((1, MIX_TILE, D_MODEL), lambda b, j: (b, j, 0)),
        out_shape=jax.ShapeDtypeStruct(x.shape, F32),
        scratch_shapes=[
            pltpu.VMEM((2, N_Q_HEADS, BLOCK, 2 * BLOCK), F32),
            pltpu.SMEM((N_Q_HEADS,), F32),
            pltpu.SMEM((1,), jnp.int32),
            pltpu.VMEM((2, MIX_TILE, D_MODEL), BF16),
            pltpu.VMEM((2, N_COL_CHUNKS, MIX_TILE, COL_CHUNK), BF16),
            pltpu.VMEM((N_KV_HEADS, BLOCK + MIX_TILE, LANES), BF16),
            pltpu.VMEM((N_KV_HEADS, BLOCK + MIX_TILE, LANES), BF16),
            pltpu.VMEM((MIX_TILE, D_MODEL), BF16),
            pltpu.VMEM((N_LANE_BLOCKS, CONV_HALO + MIX_TILE, LANES), F32),
            pltpu.VMEM((N_LANE_BLOCKS, MIX_TILE, LANES), F32),
            pltpu.VMEM((N_COL_CHUNKS, MIX_TILE, COL_CHUNK), BF16),
            pltpu.VMEM((N_COL_CHUNKS, MIX_TILE, COL_CHUNK), BF16),
        ],
        compiler_params=pltpu.CompilerParams(
            dimension_semantics=("arbitrary", "arbitrary"), vmem_limit_bytes=VMEM_LIMIT),
        name="mixer",
    )(*operands)

    tokens = batch * seq
    out = pl.pallas_call(
        _mlp_kernel,
        grid=(tokens // MLP_TILE,),
        in_specs=[pl.BlockSpec((MLP_TILE, D_MODEL), lambda i: (i, 0)),
                  _resident((1, D_MODEL)), _resident((D_FF // FF_CHUNK, D_MODEL, FF_CHUNK)),
                  _resident((D_FF // FF_CHUNK, FF_CHUNK, D_MODEL))],
        out_specs=pl.BlockSpec((MLP_TILE, D_MODEL), lambda i: (i, 0)),
        out_shape=jax.ShapeDtypeStruct((tokens, D_MODEL), F32),
        scratch_shapes=[pltpu.VMEM((MLP_TILE, D_MODEL), BF16)],
        compiler_params=pltpu.CompilerParams(
            dimension_semantics=("arbitrary",), vmem_limit_bytes=VMEM_LIMIT),
        name="mlp",
    )(x1.reshape(tokens, D_MODEL), row(norm_mlp_g[layer]),
      w_ff1[layer].astype(BF16).reshape(D_MODEL, D_FF // FF_CHUNK, FF_CHUNK).transpose(1, 0, 2),
      w_ff2[layer].astype(BF16).reshape(D_FF // FF_CHUNK, FF_CHUNK, D_MODEL))
    return out.reshape(batch, seq, D_MODEL)
```

```python
import functools

import numpy as np
import jax
import jax.numpy as jnp
from jax import lax
from jax.experimental import pallas as pl
from jax.experimental.pallas import tpu as pltpu

D_MODEL = 1024
HEAD_DIM = 64
N_Q_HEADS = 16
N_KV_HEADS = 4
KV_WIDTH = N_KV_HEADS * HEAD_DIM
GROUP = N_Q_HEADS // N_KV_HEADS
BLOCK = 128
CONV_WIDTH = 31
D_FF = 4 * D_MODEL
N_BUCKETS = 32
MAX_DISTANCE = 128
EPS = 1e-6
NEG = -1e30
LOG2E = 1.4426950408889634
QK_BOUND_MARGIN = 1.03
MAX_EXP2_SPREAD = 100.0

LANES = 128
SUBLANES = 8
N_LANE_BLOCKS = D_MODEL // LANES
COL_CHUNK = 2 * LANES
N_COL_CHUNKS = D_MODEL // COL_CHUNK
CONV_HALO = 32
CONV_SKIP = CONV_HALO - (CONV_WIDTH - 1)
CONV_ROWS = 64

MIX_TILE = 512
MLP_TILE = 1024
FF_CHUNK = 1024
VMEM_LIMIT = 56 * 1024 * 1024

F32 = jnp.float32
BF16 = jnp.bfloat16


def _bucket_table():
    qi = np.arange(BLOCK, dtype=np.int32)[:, None]
    kj = np.arange(2 * BLOCK, dtype=np.int32)[None, :]
    dist = qi + BLOCK - kj
    n = np.maximum(dist, 0)
    max_exact = N_BUCKETS // 2
    nf = np.maximum(n, 1).astype(np.float32)
    large = max_exact + (np.log(nf / np.float32(max_exact)) / np.float32(np.log(MAX_DISTANCE / max_exact))
                         * np.float32(N_BUCKETS - max_exact)).astype(np.int32)
    large = np.minimum(large, N_BUCKETS - 1)
    bucket = np.where(n < max_exact, n, large)
    valid = (dist >= 0) & (dist < BLOCK)
    return np.where(valid, bucket, -1).astype(np.int32)


def _head_mean_matrix():
    idx = np.arange(COL_CHUNK) // HEAD_DIM
    return (idx[:, None] == idx[None, :]).astype(np.float32) / HEAD_DIM


def _dot(a, b):
    return jnp.dot(a, b, preferred_element_type=F32)


def _dot_nt(a, b):
    return lax.dot_general(a, b, (((1,), (1,)), ((), ())), preferred_element_type=F32)


def _rms(x, gain):
    return x * lax.rsqrt(jnp.mean(x * x, axis=-1, keepdims=True) + EPS) * gain


def _head_rms(t, hm, gain):
    ms = _dot((t * t).astype(BF16), hm)
    return t * lax.rsqrt(ms + EPS) * gain


def _mixer_kernel(x_ref, bucket_ref, relb_ref, sink_ref, qk_bound_ref, hm_ref,
                  g_mix_ref, gq_ref, gk_ref, w_q_ref, w_kv_ref, w_a_ref, w_g_ref,
                  w_gates_ref, w_ao_ref, w_dw_ref, b_dw_ref, ln_g_ref, ln_b_ref,
                  w_co_ref, w_out_ref,
                  o_ref,
                  bias_s, sink_s, bounded_s, u_s, q_s, kdup_s, vdup_s, attn_s, h_s, conv_s, ga_s, gc_s):
    tile = x_ref.shape[1]
    n_blocks = tile // BLOCK
    b = pl.program_id(0)
    j = pl.program_id(1)

    @pl.when((b == 0) & (j == 0))
    def _build_bias():
        qk_bound = qk_bound_ref[0]
        his = [relb_ref[0, h] for h in range(N_Q_HEADS)]
        los = list(his)
        for bb in range(1, N_BUCKETS):
            his = [jnp.maximum(hi, relb_ref[bb, h]) for h, hi in enumerate(his)]
            los = [jnp.minimum(lo, relb_ref[bb, h]) for h, lo in enumerate(los)]
        spread = (his[0] - los[0]) * LOG2E
        for h in range(1, N_Q_HEADS):
            spread = jnp.maximum(spread, (his[h] - los[h]) * LOG2E)
        bounded = 2.0 * qk_bound + spread <= MAX_EXP2_SPREAD
        bounded_s[0] = bounded.astype(jnp.int32)

        bk = bucket_ref[...]
        left = lax.broadcasted_iota(jnp.int32, bk.shape, 1) < BLOCK
        for h in range(N_Q_HEADS):
            shift = jnp.where(bounded, qk_bound + his[h] * LOG2E, 0.0)
            acc = jnp.full(bk.shape, NEG, F32)
            for bb in range(N_BUCKETS):
                acc = jnp.where(bk == bb, relb_ref[bb, h] * LOG2E - shift, acc)
            bias_s[0, h] = acc
            bias_s[1, h] = jnp.where(left, NEG, acc)
            sink_s[h] = jnp.max(jnp.exp2(jnp.full((SUBLANES, LANES), sink_ref[h] * LOG2E - shift, F32)))

    @pl.when(j == 0)
    def _zero_halo():
        kdup_s[:, 0:BLOCK, :] = jnp.zeros((N_KV_HEADS, BLOCK, LANES), BF16)
        vdup_s[:, 0:BLOCK, :] = jnp.zeros((N_KV_HEADS, BLOCK, LANES), BF16)
        h_s[:, 0:CONV_HALO, :] = jnp.zeros((N_LANE_BLOCKS, CONV_HALO, LANES), F32)

    @pl.when(j != 0)
    def _carry_halo():
        kdup_s[:, 0:BLOCK, :] = kdup_s[:, tile:tile + BLOCK, :]
        vdup_s[:, 0:BLOCK, :] = vdup_s[:, tile:tile + BLOCK, :]
        h_s[:, 0:CONV_HALO, :] = h_s[:, tile:tile + CONV_HALO, :]

    x = x_ref[0]
    u = _rms(x, g_mix_ref[...]).astype(BF16)
    hm = hm_ref[...]

    hglu = _dot(u, w_a_ref[...]) * jax.nn.sigmoid(_dot(u, w_g_ref[...]))
    for c in range(N_LANE_BLOCKS):
        h_s[c, CONV_HALO:CONV_HALO + tile, :] = hglu[:, c * LANES:(c + 1) * LANES]

    kv = _dot(u, w_kv_ref[...])
    kn = _head_rms(kv[:, :KV_WIDTH], hm, gk_ref[...])
    vv = kv[:, KV_WIDTH:]
    low = lax.broadcasted_iota(jnp.int32, (tile, LANES), 1) < HEAD_DIM
    for src, dst in ((kn, kdup_s), (vv, vdup_s)):
        for c in range(KV_WIDTH // LANES):
            s0 = src[:, c * LANES:(c + 1) * LANES]
            s1 = pltpu.roll(s0, HEAD_DIM, 1)
            dst[2 * c + 0, BLOCK:BLOCK + tile, :] = jnp.where(low, s0, s1).astype(BF16)
            dst[2 * c + 1, BLOCK:BLOCK + tile, :] = jnp.where(low, s1, s0).astype(BF16)
    u_s[...] = u

    def _conv_lane_block(c):
        for r in range(tile // CONV_ROWS):
            acc = jnp.broadcast_to(b_dw_ref[c], (CONV_ROWS, LANES))
            for a in range(SUBLANES):
                taps = range(a, CONV_WIDTH, SUBLANES)
                start = r * CONV_ROWS + CONV_SKIP + a
                rows = h_s[c, start:start + CONV_ROWS + SUBLANES * (len(taps) - 1), :]
                for k, t in enumerate(taps):
                    acc = acc + rows[SUBLANES * k:SUBLANES * k + CONV_ROWS] * w_dw_ref[c, t:t + 1, :]
            conv_s[c, r * CONV_ROWS:(r + 1) * CONV_ROWS, :] = acc

    def _conv_and_q(i, carry):
        _conv_lane_block(i)
        cols = pl.ds(pl.multiple_of(i * COL_CHUNK, COL_CHUNK), COL_CHUNK)
        qn = _head_rms(_dot(u_s[...], w_q_ref[:, cols]), hm, gq_ref[i])
        even = (lax.broadcasted_iota(jnp.int32, qn.shape, 1) & HEAD_DIM) == 0
        q_s[0, i] = jnp.where(even, qn, 0.0).astype(BF16)
        q_s[1, i] = jnp.where(even, 0.0, qn).astype(BF16)
        return carry

    def _conv_and_gates(i, carry):
        _conv_lane_block(i + N_COL_CHUNKS)
        gates = _dot(u_s[...], w_gates_ref[i]).astype(BF16)
        ga_s[i] = gates[:, :COL_CHUNK]
        gc_s[i] = gates[:, COL_CHUNK:]
        return carry

    lax.fori_loop(0, N_COL_CHUNKS, _conv_and_q, 0)
    lax.fori_loop(0, N_COL_CHUNKS, _conv_and_gates, 0)

    def _attn_block(bounded, n, carry):
        r0 = pl.multiple_of(n * BLOCK, BLOCK)
        first = jnp.where((j == 0) & (n == 0), 1, 0)
        low_b = lax.broadcasted_iota(jnp.int32, (BLOCK, LANES), 1) < HEAD_DIM
        for g in range(N_KV_HEADS):
            q4 = jnp.concatenate(
                [q_s[i, g, pl.ds(r0, BLOCK), pr * LANES:(pr + 1) * LANES] for pr in range(2) for i in range(2)],
                axis=0)
            kb = kdup_s[g, pl.ds(r0, 2 * BLOCK), :]
            vb = vdup_s[g, pl.ds(r0, 2 * BLOCK), :]
            s4 = _dot_nt(q4, kb)
            es, rs = [], []
            for hh in range(GROUP):
                h = GROUP * g + hh
                s = s4[hh * BLOCK:(hh + 1) * BLOCK] + bias_s[first, h]
                if bounded:
                    e = jnp.exp2(s)
                    denom = jnp.sum(e, axis=-1, keepdims=True) + sink_s[h]
                else:
                    m = jnp.max(s, axis=-1, keepdims=True)
                    e = jnp.exp2(s - m)
                    denom = jnp.sum(e, axis=-1, keepdims=True) + jnp.exp2(sink_ref[h] * LOG2E - m)
                es.append(e.astype(BF16))
                rs.append(1.0 / denom)
            o4 = _dot(jnp.concatenate(es, axis=0), vb)
            for pr in range(2):
                o_even = o4[(2 * pr) * BLOCK:(2 * pr + 1) * BLOCK] * rs[2 * pr]
                o_odd = o4[(2 * pr + 1) * BLOCK:(2 * pr + 2) * BLOCK] * rs[2 * pr + 1]
                p = 2 * g + pr
                attn_s[pl.ds(r0, BLOCK), p * LANES:(p + 1) * LANES] = jnp.where(low_b, o_even, o_odd).astype(BF16)
        return carry

    @pl.when(bounded_s[0] == 1)
    def _attention_constant_shift():
        lax.fori_loop(0, n_blocks, functools.partial(_attn_block, True), 0, unroll=True)

    @pl.when(bounded_s[0] != 1)
    def _attention_row_max():
        lax.fori_loop(0, n_blocks, functools.partial(_attn_block, False), 0)

    gate_a = jax.nn.sigmoid(jnp.concatenate([ga_s[i] for i in range(N_COL_CHUNKS)], axis=1).astype(F32))
    merged = gate_a * _dot(attn_s[...], w_ao_ref[...])

    cv = jnp.concatenate([conv_s[c] for c in range(N_LANE_BLOCKS)], axis=1)
    mu = jnp.mean(cv, axis=-1, keepdims=True)
    xc = cv - mu
    var = jnp.mean(xc * xc, axis=-1, keepdims=True)
    y = xc * lax.rsqrt(var + EPS) * ln_g_ref[...] + ln_b_ref[...]
    conv = _dot((y * jax.nn.sigmoid(y)).astype(BF16), w_co_ref[...])

    gate_c = jax.nn.sigmoid(jnp.concatenate([gc_s[i] for i in range(N_COL_CHUNKS)], axis=1).astype(F32))
    merged = merged + gate_c * conv
    o_ref[0] = x + _dot(merged.astype(BF16), w_out_ref[...])


def _mlp_kernel(x_ref, g_ref, w1_ref, w2_ref, o_ref, xn_s):
    x = x_ref[...]
    xn_s[...] = _rms(x, g_ref[...]).astype(BF16)
    acc = x
    for c in range(D_FF // FF_CHUNK):
        cols = slice(c * FF_CHUNK, (c + 1) * FF_CHUNK)
        hmid = jnp.square(jnp.maximum(_dot(xn_s[...], w1_ref[:, cols]), 0.0))
        acc = acc + _dot(hmid.astype(BF16), w2_ref[cols, :])
    o_ref[...] = acc


def _resident(shape):
    zeros = (0,) * len(shape)
    return pl.BlockSpec(shape, lambda *_: zeros, pipeline_mode=pl.Buffered(1))


def _smem():
    return pl.BlockSpec(memory_space=pltpu.SMEM)


def kernel(x, norm_mix_g, w_in, q_norm_g, k_norm_g, attn_sinks, rel_bias, w_attn_o, w_dw, b_dw,
           conv_ln_g, conv_ln_b, w_conv_out, w_out, norm_mlp_g, w_ff1, w_ff2):
    batch, seq, d = x.shape
    assert d == D_MODEL and seq % MIX_TILE == 0 and (batch * seq) % MLP_TILE == 0
    layer = 0
    row = lambda v: v.reshape(1, -1).astype(F32)
    w = w_in[layer].astype(BF16)
    q_end = N_Q_HEADS * HEAD_DIM
    k_end = q_end + KV_WIDTH
    v_end = k_end + KV_WIDTH
    a_end = v_end + D_MODEL
    g_end = a_end + D_MODEL
    ga_end = g_end + D_MODEL
    lane_blocks = lambda v: v.astype(F32).reshape(-1, N_LANE_BLOCKS, LANES).transpose(1, 0, 2)
    col_chunks = lambda m: m.reshape(m.shape[0], N_COL_CHUNKS, COL_CHUNK).transpose(1, 0, 2)
    q_gain = jnp.tile(q_norm_g[layer].astype(F32), N_Q_HEADS) * (HEAD_DIM ** -0.5 * LOG2E)
    qk_bound = (QK_BOUND_MARGIN * HEAD_DIM * jnp.max(jnp.abs(q_gain))
                * jnp.max(jnp.abs(k_norm_g[layer].astype(F32))))

    operands = [
        x,
        jnp.asarray(_bucket_table()),
        rel_bias.astype(F32),
        attn_sinks[layer].astype(F32),
        qk_bound.reshape(1),
        jnp.asarray(_head_mean_matrix(), dtype=BF16),
        row(norm_mix_g[layer]),
        q_gain.reshape(N_COL_CHUNKS, 1, COL_CHUNK),
        row(jnp.tile(k_norm_g[layer], N_KV_HEADS)),
        w[:, :q_end], w[:, q_end:v_end], w[:, v_end:a_end], w[:, a_end:g_end],
        jnp.concatenate([col_chunks(w[:, g_end:ga_end]), col_chunks(w[:, ga_end:])], axis=2),
        w_attn_o[layer].astype(BF16),
        lane_blocks(w_dw[layer]),
        lane_blocks(b_dw[layer].reshape(1, -1)),
        row(conv_ln_g[layer]), row(conv_ln_b[layer]),
        w_conv_out[layer].astype(BF16),
        w_out[layer].astype(BF16),
    ]
    in_specs = [pl.BlockSpec((1, MIX_TILE, D_MODEL), lambda b, j: (b, j, 0)),
                _resident(operands[1].shape), _smem(), _smem(), _smem()]
    in_specs += [_resident(op.shape) for op in operands[5:]]

    x1 = pl.pallas_call(
        _mixer_kernel,
        grid=(batch, seq // MIX_TILE),
        in_specs=in_specs,
        out_specs=pl.BlockSpec((1, MIX_TILE, D_MODEL), lambda b, j: (b, j, 0)),
        out_shape=jax.ShapeDtypeStruct(x.shape, F32),
        scratch_shapes=[
            pltpu.VMEM((2, N_Q_HEADS, BLOCK, 2 * BLOCK), F32),
            pltpu.SMEM((N_Q_HEADS,), F32),
            pltpu.SMEM((1,), jnp.int32),
            pltpu.VMEM((MIX_TILE, D_MODEL), BF16),
            pltpu.VMEM((2, N_COL_CHUNKS, MIX_TILE, COL_CHUNK), BF16),
            pltpu.VMEM((N_KV_HEADS, BLOCK + MIX_TILE, LANES), BF16),
            pltpu.VMEM((N_KV_HEADS, BLOCK + MIX_TILE, LANES), BF16),
            pltpu.VMEM((MIX_TILE, D_MODEL), BF16),
            pltpu.VMEM((N_LANE_BLOCKS, CONV_HALO + MIX_TILE, LANES), F32),
            pltpu.VMEM((N_LANE_BLOCKS, MIX_TILE, LANES), F32),
            pltpu.VMEM((N_COL_CHUNKS, MIX_TILE, COL_CHUNK), BF16),
            pltpu.VMEM((N_COL_CHUNKS, MIX_TILE, COL_CHUNK), BF16),
        ],
        compiler_params=pltpu.CompilerParams(
            dimension_semantics=("arbitrary", "arbitrary"), vmem_limit_bytes=VMEM_LIMIT),
        name="mixer",
    )(*operands)

    tokens = batch * seq
    out = pl.pallas_call(
        _mlp_kernel,
        grid=(tokens // MLP_TILE,),
        in_specs=[pl.BlockSpec((MLP_TILE, D_MODEL), lambda i: (i, 0)),
                  _resident((1, D_MODEL)), _resident((D_MODEL, D_FF)), _resident((D_FF, D_MODEL))],
        out_specs=pl.BlockSpec((MLP_TILE, D_MODEL), lambda i: (i, 0)),
        out_shape=jax.ShapeDtypeStruct((tokens, D_MODEL), F32),
        scratch_shapes=[pltpu.VMEM((MLP_TILE, D_MODEL), BF16)],
        compiler_params=pltpu.CompilerParams(
            dimension_semantics=("arbitrary",), vmem_limit_bytes=VMEM_LIMIT),
        name="mlp",
    )(x1.reshape(tokens, D_MODEL), row(norm_mlp_g[layer]),
      w_ff1[layer].astype(BF16), w_ff2[layer].astype(BF16))
    return out.reshape(batch, seq, D_MODEL)
```

```python
import functools

import numpy as np
import jax
import jax.numpy as jnp
from jax import lax
from jax.experimental import pallas as pl
from jax.experimental.pallas import tpu as pltpu

D_MODEL = 1024
HEAD_DIM = 64
N_Q_HEADS = 16
N_KV_HEADS = 4
KV_WIDTH = N_KV_HEADS * HEAD_DIM
GROUP = N_Q_HEADS // N_KV_HEADS
BLOCK = 128
CONV_WIDTH = 31
D_FF = 4 * D_MODEL
N_BUCKETS = 32
MAX_DISTANCE = 128
EPS = 1e-6
NEG = -1e30
LOG2E = 1.4426950408889634
QK_BOUND_MARGIN = 1.03
MAX_EXP2_SPREAD = 100.0

LANES = 128
SUBLANES = 8
N_LANE_BLOCKS = D_MODEL // LANES
COL_CHUNK = 2 * LANES
N_COL_CHUNKS = D_MODEL // COL_CHUNK
CONV_HALO = 32
CONV_SKIP = CONV_HALO - (CONV_WIDTH - 1)
CONV_ROWS = 64

MIX_TILE = 512
MLP_TILE = 1024
FF_CHUNK = 1024
VMEM_LIMIT = 56 * 1024 * 1024

F32 = jnp.float32
BF16 = jnp.bfloat16


def _bucket_table():
    qi = np.arange(BLOCK, dtype=np.int32)[:, None]
    kj = np.arange(2 * BLOCK, dtype=np.int32)[None, :]
    dist = qi + BLOCK - kj
    n = np.maximum(dist, 0)
    max_exact = N_BUCKETS // 2
    nf = np.maximum(n, 1).astype(np.float32)
    large = max_exact + (np.log(nf / np.float32(max_exact)) / np.float32(np.log(MAX_DISTANCE / max_exact))
                         * np.float32(N_BUCKETS - max_exact)).astype(np.int32)
    large = np.minimum(large, N_BUCKETS - 1)
    bucket = np.where(n < max_exact, n, large)
    valid = (dist >= 0) & (dist < BLOCK)
    return np.where(valid, bucket, -1).astype(np.int32)


def _head_mean_matrix():
    idx = np.arange(COL_CHUNK) // HEAD_DIM
    return (idx[:, None] == idx[None, :]).astype(np.float32) / HEAD_DIM


def _dot(a, b):
    return jnp.dot(a, b, preferred_element_type=F32)


def _dot_nt(a, b):
    return lax.dot_general(a, b, (((1,), (1,)), ((), ())), preferred_element_type=F32)


def _rms(x, gain):
    return x * lax.rsqrt(jnp.mean(x * x, axis=-1, keepdims=True) + EPS) * gain


def _head_rms(t, hm, gain):
    ms = _dot((t * t).astype(BF16), hm)
    return t * lax.rsqrt(ms + EPS) * gain


def _mixer_kernel(x_ref, bucket_ref, relb_ref, sink_ref, qk_bound_ref, hm_ref,
                  g_mix_ref, gq_ref, gk_ref, w_q_ref, w_kv_ref, w_a_ref, w_g_ref,
                  w_gates_ref, w_ao_ref, w_dw_ref, b_dw_ref, ln_g_ref, ln_b_ref,
                  w_co_ref, w_out_ref,
                  o_ref,
                  bias_s, sink_s, bounded_s, u_s, q_s, kdup_s, vdup_s, attn_s, h_s, conv_s, ga_s, gc_s):
    tile = x_ref.shape[1]
    n_blocks = tile // BLOCK
    b = pl.program_id(0)
    j = pl.program_id(1)

    @pl.when((b == 0) & (j == 0))
    def _build_bias():
        qk_bound = qk_bound_ref[0]
        his = [relb_ref[0, h] for h in range(N_Q_HEADS)]
        los = list(his)
        for bb in range(1, N_BUCKETS):
            his = [jnp.maximum(hi, relb_ref[bb, h]) for h, hi in enumerate(his)]
            los = [jnp.minimum(lo, relb_ref[bb, h]) for h, lo in enumerate(los)]
        spread = (his[0] - los[0]) * LOG2E
        for h in range(1, N_Q_HEADS):
            spread = jnp.maximum(spread, (his[h] - los[h]) * LOG2E)
        bounded = 2.0 * qk_bound + spread <= MAX_EXP2_SPREAD
        bounded_s[0] = bounded.astype(jnp.int32)

        bk = bucket_ref[...]
        left = lax.broadcasted_iota(jnp.int32, bk.shape, 1) < BLOCK
        for h in range(N_Q_HEADS):
            shift = jnp.where(bounded, qk_bound + his[h] * LOG2E, 0.0)
            acc = jnp.full(bk.shape, NEG, F32)
            for bb in range(N_BUCKETS):
                acc = jnp.where(bk == bb, relb_ref[bb, h] * LOG2E - shift, acc)
            bias_s[0, h] = acc
            bias_s[1, h] = jnp.where(left, NEG, acc)
            sink_s[h] = jnp.max(jnp.exp2(jnp.full((SUBLANES, LANES), sink_ref[h] * LOG2E - shift, F32)))

    @pl.when(j == 0)
    def _zero_halo():
        kdup_s[:, 0:BLOCK, :] = jnp.zeros((N_KV_HEADS, BLOCK, LANES), BF16)
        vdup_s[:, 0:BLOCK, :] = jnp.zeros((N_KV_HEADS, BLOCK, LANES), BF16)
        h_s[:, 0:CONV_HALO, :] = jnp.zeros((N_LANE_BLOCKS, CONV_HALO, LANES), F32)

    @pl.when(j != 0)
    def _carry_halo():
        kdup_s[:, 0:BLOCK, :] = kdup_s[:, tile:tile + BLOCK, :]
        vdup_s[:, 0:BLOCK, :] = vdup_s[:, tile:tile + BLOCK, :]
        h_s[:, 0:CONV_HALO, :] = h_s[:, tile:tile + CONV_HALO, :]

    x = x_ref[0]
    u = _rms(x, g_mix_ref[...]).astype(BF16)
    hm = hm_ref[...]

    hglu = _dot(u, w_a_ref[...]) * jax.nn.sigmoid(_dot(u, w_g_ref[...]))
    for c in range(N_LANE_BLOCKS):
        h_s[c, CONV_HALO:CONV_HALO + tile, :] = hglu[:, c * LANES:(c + 1) * LANES]

    kv = _dot(u, w_kv_ref[...])
    kn = _head_rms(kv[:, :KV_WIDTH], hm, gk_ref[...])
    vv = kv[:, KV_WIDTH:]
    low = lax.broadcasted_iota(jnp.int32, (tile, LANES), 1) < HEAD_DIM
    for src, dst in ((kn, kdup_s), (vv, vdup_s)):
        for c in range(KV_WIDTH // LANES):
            s0 = src[:, c * LANES:(c + 1) * LANES]
            s1 = pltpu.roll(s0, HEAD_DIM, 1)
            dst[2 * c + 0, BLOCK:BLOCK + tile, :] = jnp.where(low, s0, s1).astype(BF16)
            dst[2 * c + 1, BLOCK:BLOCK + tile, :] = jnp.where(low, s1, s0).astype(BF16)
    u_s[...] = u

    def _conv_lane_block(c):
        for r in range(tile // CONV_ROWS):
            acc = jnp.broadcast_to(b_dw_ref[c], (CONV_ROWS, LANES))
            for a in range(SUBLANES):
                taps = range(a, CONV_WIDTH, SUBLANES)
                start = r * CONV_ROWS + CONV_SKIP + a
                rows = h_s[c, start:start + CONV_ROWS + SUBLANES * (len(taps) - 1), :]
                for k, t in enumerate(taps):
                    acc = acc + rows[SUBLANES * k:SUBLANES * k + CONV_ROWS] * w_dw_ref[c, t:t + 1, :]
            conv_s[c, r * CONV_ROWS:(r + 1) * CONV_ROWS, :] = acc

    def _conv_and_q(i, carry):
        _conv_lane_block(i)
        qn = _head_rms(_dot(u_s[...], w_q_ref[i]), hm, gq_ref[i])
        even = (lax.broadcasted_iota(jnp.int32, qn.shape, 1) & HEAD_DIM) == 0
        q_s[0, i] = jnp.where(even, qn, 0.0).astype(BF16)
        q_s[1, i] = jnp.where(even, 0.0, qn).astype(BF16)
        return carry

    def _conv_and_gates(i, carry):
        _conv_lane_block(i + N_COL_CHUNKS)
        gates = _dot(u_s[...], w_gates_ref[i]).astype(BF16)
        ga_s[i] = gates[:, :COL_CHUNK]
        gc_s[i] = gates[:, COL_CHUNK:]
        return carry

    lax.fori_loop(0, N_COL_CHUNKS, _conv_and_q, 0)
    lax.fori_loop(0, N_COL_CHUNKS, _conv_and_gates, 0)

    def _attn_block(bounded, n, carry):
        r0 = pl.multiple_of(n * BLOCK, BLOCK)
        first = jnp.where((j == 0) & (n == 0), 1, 0)
        low_b = lax.broadcasted_iota(jnp.int32, (BLOCK, LANES), 1) < HEAD_DIM
        for g in range(N_KV_HEADS):
            q4 = jnp.concatenate(
                [q_s[i, g, pl.ds(r0, BLOCK), pr * LANES:(pr + 1) * LANES] for pr in range(2) for i in range(2)],
                axis=0)
            kb = kdup_s[g, pl.ds(r0, 2 * BLOCK), :]
            vb = vdup_s[g, pl.ds(r0, 2 * BLOCK), :]
            s4 = _dot_nt(q4, kb)
            es, rs = [], []
            for hh in range(GROUP):
                h = GROUP * g + hh
                s = s4[hh * BLOCK:(hh + 1) * BLOCK] + bias_s[first, h]
                if bounded:
                    e = jnp.exp2(s)
                    denom = jnp.sum(e, axis=-1, keepdims=True) + sink_s[h]
                else:
                    m = jnp.max(s, axis=-1, keepdims=True)
                    e = jnp.exp2(s - m)
                    denom = jnp.sum(e, axis=-1, keepdims=True) + jnp.exp2(sink_ref[h] * LOG2E - m)
                es.append(e.astype(BF16))
                rs.append(1.0 / denom)
            o4 = _dot(jnp.concatenate(es, axis=0), vb)
            for pr in range(2):
                o_even = o4[(2 * pr) * BLOCK:(2 * pr + 1) * BLOCK] * rs[2 * pr]
                o_odd = o4[(2 * pr + 1) * BLOCK:(2 * pr + 2) * BLOCK] * rs[2 * pr + 1]
                p = 2 * g + pr
                attn_s[pl.ds(r0, BLOCK), p * LANES:(p + 1) * LANES] = jnp.where(low_b, o_even, o_odd).astype(BF16)
        return carry

    @pl.when(bounded_s[0] == 1)
    def _attention_constant_shift():
        lax.fori_loop(0, n_blocks, functools.partial(_attn_block, True), 0, unroll=True)

    @pl.when(bounded_s[0] != 1)
    def _attention_row_max():
        lax.fori_loop(0, n_blocks, functools.partial(_attn_block, False), 0)

    gate_a = jax.nn.sigmoid(jnp.concatenate([ga_s[i] for i in range(N_COL_CHUNKS)], axis=1).astype(F32))
    merged = gate_a * _dot(attn_s[...], w_ao_ref[...])

    cv = jnp.concatenate([conv_s[c] for c in range(N_LANE_BLOCKS)], axis=1)
    mu = jnp.mean(cv, axis=-1, keepdims=True)
    xc = cv - mu
    var = jnp.mean(xc * xc, axis=-1, keepdims=True)
    y = xc * lax.rsqrt(var + EPS) * ln_g_ref[...] + ln_b_ref[...]
    conv = _dot((y * jax.nn.sigmoid(y)).astype(BF16), w_co_ref[...])

    gate_c = jax.nn.sigmoid(jnp.concatenate([gc_s[i] for i in range(N_COL_CHUNKS)], axis=1).astype(F32))
    merged = merged + gate_c * conv
    o_ref[0] = x + _dot(merged.astype(BF16), w_out_ref[...])


def _mlp_kernel(x_ref, g_ref, w1_ref, w2_ref, o_ref, xn_s):
    x = x_ref[...]
    xn_s[...] = _rms(x, g_ref[...]).astype(BF16)
    acc = x
    for c in range(D_FF // FF_CHUNK):
        cols = slice(c * FF_CHUNK, (c + 1) * FF_CHUNK)
        hmid = jnp.square(jnp.maximum(_dot(xn_s[...], w1_ref[:, cols]), 0.0))
        acc = acc + _dot(hmid.astype(BF16), w2_ref[cols, :])
    o_ref[...] = acc


def _resident(shape):
    zeros = (0,) * len(shape)
    return pl.BlockSpec(shape, lambda *_: zeros, pipeline_mode=pl.Buffered(1))


def _smem():
    return pl.BlockSpec(memory_space=pltpu.SMEM)


def kernel(x, norm_mix_g, w_in, q_norm_g, k_norm_g, attn_sinks, rel_bias, w_attn_o, w_dw, b_dw,
           conv_ln_g, conv_ln_b, w_conv_out, w_out, norm_mlp_g, w_ff1, w_ff2):
    batch, seq, d = x.shape
    assert d == D_MODEL and seq % MIX_TILE == 0 and (batch * seq) % MLP_TILE == 0
    layer = 0
    row = lambda v: v.reshape(1, -1).astype(F32)
    w = w_in[layer].astype(BF16)
    q_end = N_Q_HEADS * HEAD_DIM
    k_end = q_end + KV_WIDTH
    v_end = k_end + KV_WIDTH
    a_end = v_end + D_MODEL
    g_end = a_end + D_MODEL
    ga_end = g_end + D_MODEL
    lane_blocks = lambda v: v.astype(F32).reshape(-1, N_LANE_BLOCKS, LANES).transpose(1, 0, 2)
    col_chunks = lambda m: m.reshape(m.shape[0], N_COL_CHUNKS, COL_CHUNK).transpose(1, 0, 2)
    q_gain = jnp.tile(q_norm_g[layer].astype(F32), N_Q_HEADS) * (HEAD_DIM ** -0.5 * LOG2E)
    qk_bound = (QK_BOUND_MARGIN * HEAD_DIM * jnp.max(jnp.abs(q_gain))
                * jnp.max(jnp.abs(k_norm_g[layer].astype(F32))))

    operands = [
        x,
        jnp.asarray(_bucket_table()),
        rel_bias.astype(F32),
        attn_sinks[layer].astype(F32),
        qk_bound.reshape(1),
        jnp.asarray(_head_mean_matrix(), dtype=BF16),
        row(norm_mix_g[layer]),
        q_gain.reshape(N_COL_CHUNKS, 1, COL_CHUNK),
        row(jnp.tile(k_norm_g[layer], N_KV_HEADS)),
        col_chunks(w[:, :q_end]), w[:, q_end:v_end], w[:, v_end:a_end], w[:, a_end:g_end],
        jnp.concatenate([col_chunks(w[:, g_end:ga_end]), col_chunks(w[:, ga_end:])], axis=2),
        w_attn_o[layer].astype(BF16),
        lane_blocks(w_dw[layer]),
        lane_blocks(b_dw[layer].reshape(1, -1)),
        row(conv_ln_g[layer]), row(conv_ln_b[layer]),
        w_conv_out[layer].astype(BF16),
        w_out[layer].astype(BF16),
    ]
    in_specs = [pl.BlockSpec((1, MIX_TILE, D_MODEL), lambda b, j: (b, j, 0)),
                _resident(operands[1].shape), _smem(), _smem(), _smem()]
    in_specs += [_resident(op.shape) for op in operands[5:]]

    x1 = pl.pallas_call(
        _mixer_kernel,
        grid=(batch, seq // MIX_TILE),
        in_specs=in_specs,
        out_specs=pl.BlockSpec((1, MIX_TILE, D_MODEL), lambda b, j: (b, j, 0)),
        out_shape=jax.ShapeDtypeStruct(x.shape, F32),
        scratch_shapes=[
            pltpu.VMEM((2, N_Q_HEADS, BLOCK, 2 * BLOCK), F32),
            pltpu.SMEM((N_Q_HEADS,), F32),
            pltpu.SMEM((1,), jnp.int32),
            pltpu.VMEM((MIX_TILE, D_MODEL), BF16),
            pltpu.VMEM((2, N_COL_CHUNKS, MIX_TILE, COL_CHUNK), BF16),
            pltpu.VMEM((N_KV_HEADS, BLOCK + MIX_TILE, LANES), BF16),
            pltpu.VMEM((N_KV_HEADS, BLOCK + MIX_TILE, LANES), BF16),
            pltpu.VMEM((MIX_TILE, D_MODEL), BF16),
            pltpu.VMEM((N_LANE_BLOCKS, CONV_HALO + MIX_TILE, LANES), F32),
            pltpu.VMEM((N_LANE_BLOCKS, MIX_TILE, LANES), F32),
            pltpu.VMEM((N_COL_CHUNKS, MIX_TILE, COL_CHUNK), BF16),
            pltpu.VMEM((N_COL_CHUNKS, MIX_TILE, COL_CHUNK), BF16),
        ],
        compiler_params=pltpu.CompilerParams(
            dimension_semantics=("arbitrary", "arbitrary"), vmem_limit_bytes=VMEM_LIMIT),
        name="mixer",
    )(*operands)

    tokens = batch * seq
    out = pl.pallas_call(
        _mlp_kernel,
        grid=(tokens // MLP_TILE,),
        in_specs=[pl.BlockSpec((MLP_TILE, D_MODEL), lambda i: (i, 0)),
                  _resident((1, D_MODEL)), _resident((D_MODEL, D_FF)), _resident((D_FF, D_MODEL))],
        out_specs=pl.BlockSpec((MLP_TILE, D_MODEL), lambda i: (i, 0)),
        out_shape=jax.ShapeDtypeStruct((tokens, D_MODEL), F32),
        scratch_shapes=[pltpu.VMEM((MLP_TILE, D_MODEL), BF16)],
        compiler_params=pltpu.CompilerParams(
            dimension_semantics=("arbitrary",), vmem_limit_bytes=VMEM_LIMIT),
        name="mlp",
    )(x1.reshape(tokens, D_MODEL), row(norm_mlp_g[layer]),
      w_ff1[layer].astype(BF16), w_ff2[layer].astype(BF16))
    return out.reshape(batch, seq, D_MODEL)
```

```python
import functools

import numpy as np
import jax
import jax.numpy as jnp
from jax import lax
from jax.experimental import pallas as pl
from jax.experimental.pallas import tpu as pltpu

D_MODEL = 1024
HEAD_DIM = 64
N_Q_HEADS = 16
N_KV_HEADS = 4
KV_WIDTH = N_KV_HEADS * HEAD_DIM
GROUP = N_Q_HEADS // N_KV_HEADS
BLOCK = 128
CONV_WIDTH = 31
D_FF = 4 * D_MODEL
N_BUCKETS = 32
MAX_DISTANCE = 128
EPS = 1e-6
NEG = -1e30
LOG2E = 1.4426950408889634
QK_BOUND_MARGIN = 1.03
MAX_EXP2_SPREAD = 100.0

LANES = 128
SUBLANES = 8
N_LANE_BLOCKS = D_MODEL // LANES
COL_CHUNK = 2 * LANES
N_COL_CHUNKS = D_MODEL // COL_CHUNK
CONV_HALO = 32
CONV_SKIP = CONV_HALO - (CONV_WIDTH - 1)
CONV_ROWS = 64

MIX_TILE = 512
MLP_TILE = 1024
FF_CHUNK = 2048
VMEM_LIMIT = 56 * 1024 * 1024

F32 = jnp.float32
BF16 = jnp.bfloat16


def _bucket_table():
    qi = np.arange(BLOCK, dtype=np.int32)[:, None]
    kj = np.arange(2 * BLOCK, dtype=np.int32)[None, :]
    dist = qi + BLOCK - kj
    n = np.maximum(dist, 0)
    max_exact = N_BUCKETS // 2
    nf = np.maximum(n, 1).astype(np.float32)
    large = max_exact + (np.log(nf / np.float32(max_exact)) / np.float32(np.log(MAX_DISTANCE / max_exact))
                         * np.float32(N_BUCKETS - max_exact)).astype(np.int32)
    large = np.minimum(large, N_BUCKETS - 1)
    bucket = np.where(n < max_exact, n, large)
    valid = (dist >= 0) & (dist < BLOCK)
    return np.where(valid, bucket, -1).astype(np.int32)


def _head_mean_matrix():
    idx = np.arange(COL_CHUNK) // HEAD_DIM
    return (idx[:, None] == idx[None, :]).astype(np.float32) / HEAD_DIM


def _dot(a, b):
    return jnp.dot(a, b, preferred_element_type=F32)


def _dot_nt(a, b):
    return lax.dot_general(a, b, (((1,), (1,)), ((), ())), preferred_element_type=F32)


def _rms(x, gain):
    return x * lax.rsqrt(jnp.mean(x * x, axis=-1, keepdims=True) + EPS) * gain


def _head_rms(t, hm, gain):
    ms = _dot((t * t).astype(BF16), hm)
    return t * lax.rsqrt(ms + EPS) * gain


def _mixer_kernel(x_ref, bucket_ref, relb_ref, sink_ref, qk_bound_ref, hm_ref,
                  g_mix_ref, gq_ref, gk_ref, w_q_ref, w_kv_ref, w_a_ref, w_g_ref,
                  w_gates_ref, w_ao_ref, w_dw_ref, b_dw_ref, ln_g_ref, ln_b_ref,
                  w_co_ref, w_out_ref,
                  o_ref,
                  bias_s, sink_s, bounded_s, u_s, q_s, kdup_s, vdup_s, attn_s, h_s, conv_s, ga_s, gc_s):
    tile = x_ref.shape[1]
    n_blocks = tile // BLOCK
    b = pl.program_id(0)
    j = pl.program_id(1)

    @pl.when((b == 0) & (j == 0))
    def _build_bias():
        qk_bound = qk_bound_ref[0]
        his = [relb_ref[0, h] for h in range(N_Q_HEADS)]
        los = list(his)
        for bb in range(1, N_BUCKETS):
            his = [jnp.maximum(hi, relb_ref[bb, h]) for h, hi in enumerate(his)]
            los = [jnp.minimum(lo, relb_ref[bb, h]) for h, lo in enumerate(los)]
        spread = (his[0] - los[0]) * LOG2E
        for h in range(1, N_Q_HEADS):
            spread = jnp.maximum(spread, (his[h] - los[h]) * LOG2E)
        bounded = 2.0 * qk_bound + spread <= MAX_EXP2_SPREAD
        bounded_s[0] = bounded.astype(jnp.int32)

        bk = bucket_ref[...]
        left = lax.broadcasted_iota(jnp.int32, bk.shape, 1) < BLOCK
        for h in range(N_Q_HEADS):
            shift = jnp.where(bounded, qk_bound + his[h] * LOG2E, 0.0)
            acc = jnp.full(bk.shape, NEG, F32)
            for bb in range(N_BUCKETS):
                acc = jnp.where(bk == bb, relb_ref[bb, h] * LOG2E - shift, acc)
            bias_s[0, h] = acc
            bias_s[1, h] = jnp.where(left, NEG, acc)
            sink_s[h] = jnp.max(jnp.exp2(jnp.full((SUBLANES, LANES), sink_ref[h] * LOG2E - shift, F32)))

    @pl.when(j == 0)
    def _zero_halo():
        kdup_s[:, 0:BLOCK, :] = jnp.zeros((N_KV_HEADS, BLOCK, LANES), BF16)
        vdup_s[:, 0:BLOCK, :] = jnp.zeros((N_KV_HEADS, BLOCK, LANES), BF16)
        h_s[:, 0:CONV_HALO, :] = jnp.zeros((N_LANE_BLOCKS, CONV_HALO, LANES), F32)

    @pl.when(j != 0)
    def _carry_halo():
        kdup_s[:, 0:BLOCK, :] = kdup_s[:, tile:tile + BLOCK, :]
        vdup_s[:, 0:BLOCK, :] = vdup_s[:, tile:tile + BLOCK, :]
        h_s[:, 0:CONV_HALO, :] = h_s[:, tile:tile + CONV_HALO, :]

    x = x_ref[0]
    u = _rms(x, g_mix_ref[...]).astype(BF16)
    hm = hm_ref[...]

    hglu = _dot(u, w_a_ref[...]) * jax.nn.sigmoid(_dot(u, w_g_ref[...]))
    for c in range(N_LANE_BLOCKS):
        h_s[c, CONV_HALO:CONV_HALO + tile, :] = hglu[:, c * LANES:(c + 1) * LANES]

    kv = _dot(u, w_kv_ref[...])
    kn = _head_rms(kv[:, :KV_WIDTH], hm, gk_ref[...])
    vv = kv[:, KV_WIDTH:]
    low = lax.broadcasted_iota(jnp.int32, (tile, LANES), 1) < HEAD_DIM
    for src, dst in ((kn, kdup_s), (vv, vdup_s)):
        for c in range(KV_WIDTH // LANES):
            s0 = src[:, c * LANES:(c + 1) * LANES]
            s1 = pltpu.roll(s0, HEAD_DIM, 1)
            dst[2 * c + 0, BLOCK:BLOCK + tile, :] = jnp.where(low, s0, s1).astype(BF16)
            dst[2 * c + 1, BLOCK:BLOCK + tile, :] = jnp.where(low, s1, s0).astype(BF16)
    u_s[...] = u

    def _conv_lane_block(c):
        for r in range(tile // CONV_ROWS):
            acc = jnp.broadcast_to(b_dw_ref[c], (CONV_ROWS, LANES))
            for a in range(SUBLANES):
                taps = range(a, CONV_WIDTH, SUBLANES)
                start = r * CONV_ROWS + CONV_SKIP + a
                rows = h_s[c, start:start + CONV_ROWS + SUBLANES * (len(taps) - 1), :]
                for k, t in enumerate(taps):
                    acc = acc + rows[SUBLANES * k:SUBLANES * k + CONV_ROWS] * w_dw_ref[c, t:t + 1, :]
            conv_s[c, r * CONV_ROWS:(r + 1) * CONV_ROWS, :] = acc

    def _conv_and_q(i, carry):
        _conv_lane_block(i)
        qn = _head_rms(_dot(u_s[...], w_q_ref[i]), hm, gq_ref[i])
        even = (lax.broadcasted_iota(jnp.int32, qn.shape, 1) & HEAD_DIM) == 0
        q_s[0, i] = jnp.where(even, qn, 0.0).astype(BF16)
        q_s[1, i] = jnp.where(even, 0.0, qn).astype(BF16)
        return carry

    def _conv_and_gates(i, carry):
        _conv_lane_block(i + N_COL_CHUNKS)
        gates = _dot(u_s[...], w_gates_ref[i]).astype(BF16)
        ga_s[i] = gates[:, :COL_CHUNK]
        gc_s[i] = gates[:, COL_CHUNK:]
        return carry

    lax.fori_loop(0, N_COL_CHUNKS, _conv_and_q, 0)
    lax.fori_loop(0, N_COL_CHUNKS, _conv_and_gates, 0)

    def _attn_block(bounded, n, carry):
        r0 = pl.multiple_of(n * BLOCK, BLOCK)
        first = jnp.where((j == 0) & (n == 0), 1, 0)
        low_b = lax.broadcasted_iota(jnp.int32, (BLOCK, LANES), 1) < HEAD_DIM
        for g in range(N_KV_HEADS):
            q4 = jnp.concatenate(
                [q_s[i, g, pl.ds(r0, BLOCK), pr * LANES:(pr + 1) * LANES] for pr in range(2) for i in range(2)],
                axis=0)
            kb = kdup_s[g, pl.ds(r0, 2 * BLOCK), :]
            vb = vdup_s[g, pl.ds(r0, 2 * BLOCK), :]
            s4 = _dot_nt(q4, kb)
            es, rs = [], []
            for hh in range(GROUP):
                h = GROUP * g + hh
                s = s4[hh * BLOCK:(hh + 1) * BLOCK] + bias_s[first, h]
                if bounded:
                    e = jnp.exp2(s)
                    denom = jnp.sum(e, axis=-1, keepdims=True) + sink_s[h]
                else:
                    m = jnp.max(s, axis=-1, keepdims=True)
                    e = jnp.exp2(s - m)
                    denom = jnp.sum(e, axis=-1, keepdims=True) + jnp.exp2(sink_ref[h] * LOG2E - m)
                es.append(e.astype(BF16))
                rs.append(1.0 / denom)
            o4 = _dot(jnp.concatenate(es, axis=0), vb)
            for pr in range(2):
                o_even = o4[(2 * pr) * BLOCK:(2 * pr + 1) * BLOCK] * rs[2 * pr]
                o_odd = o4[(2 * pr + 1) * BLOCK:(2 * pr + 2) * BLOCK] * rs[2 * pr + 1]
                p = 2 * g + pr
                attn_s[pl.ds(r0, BLOCK), p * LANES:(p + 1) * LANES] = jnp.where(low_b, o_even, o_odd).astype(BF16)
        return carry

    @pl.when(bounded_s[0] == 1)
    def _attention_constant_shift():
        lax.fori_loop(0, n_blocks, functools.partial(_attn_block, True), 0, unroll=True)

    @pl.when(bounded_s[0] != 1)
    def _attention_row_max():
        lax.fori_loop(0, n_blocks, functools.partial(_attn_block, False), 0)

    gate_a = jax.nn.sigmoid(jnp.concatenate([ga_s[i] for i in range(N_COL_CHUNKS)], axis=1).astype(F32))
    merged = gate_a * _dot(attn_s[...], w_ao_ref[...])

    cv = jnp.concatenate([conv_s[c] for c in range(N_LANE_BLOCKS)], axis=1)
    mu = jnp.mean(cv, axis=-1, keepdims=True)
    xc = cv - mu
    var = jnp.mean(xc * xc, axis=-1, keepdims=True)
    y = xc * lax.rsqrt(var + EPS) * ln_g_ref[...] + ln_b_ref[...]
    conv = _dot((y * jax.nn.sigmoid(y)).astype(BF16), w_co_ref[...])

    gate_c = jax.nn.sigmoid(jnp.concatenate([gc_s[i] for i in range(N_COL_CHUNKS)], axis=1).astype(F32))
    merged = merged + gate_c * conv
    o_ref[0] = x + _dot(merged.astype(BF16), w_out_ref[...])


def _mlp_kernel(x_ref, g_ref, w1_ref, w2_ref, o_ref, xn_s):
    x = x_ref[...]
    xn_s[...] = _rms(x, g_ref[...]).astype(BF16)
    acc = x
    for c in range(D_FF // FF_CHUNK):
        cols = slice(c * FF_CHUNK, (c + 1) * FF_CHUNK)
        hmid = jnp.square(jnp.maximum(_dot(xn_s[...], w1_ref[:, cols]), 0.0))
        acc = acc + _dot(hmid.astype(BF16), w2_ref[cols, :])
    o_ref[...] = acc


def _resident(shape):
    zeros = (0,) * len(shape)
    return pl.BlockSpec(shape, lambda *_: zeros, pipeline_mode=pl.Buffered(1))


def _smem():
    return pl.BlockSpec(memory_space=pltpu.SMEM)


def kernel(x, norm_mix_g, w_in, q_norm_g, k_norm_g, attn_sinks, rel_bias, w_attn_o, w_dw, b_dw,
           conv_ln_g, conv_ln_b, w_conv_out, w_out, norm_mlp_g, w_ff1, w_ff2):
    batch, seq, d = x.shape
    assert d == D_MODEL and seq % MIX_TILE == 0 and (batch * seq) % MLP_TILE == 0
    layer = 0
    row = lambda v: v.reshape(1, -1).astype(F32)
    w = w_in[layer].astype(BF16)
    q_end = N_Q_HEADS * HEAD_DIM
    k_end = q_end + KV_WIDTH
    v_end = k_end + KV_WIDTH
    a_end = v_end + D_MODEL
    g_end = a_end + D_MODEL
    ga_end = g_end + D_MODEL
    lane_blocks = lambda v: v.astype(F32).reshape(-1, N_LANE_BLOCKS, LANES).transpose(1, 0, 2)
    col_chunks = lambda m: m.reshape(m.shape[0], N_COL_CHUNKS, COL_CHUNK).transpose(1, 0, 2)
    q_gain = jnp.tile(q_norm_g[layer].astype(F32), N_Q_HEADS) * (HEAD_DIM ** -0.5 * LOG2E)
    qk_bound = (QK_BOUND_MARGIN * HEAD_DIM * jnp.max(jnp.abs(q_gain))
                * jnp.max(jnp.abs(k_norm_g[layer].astype(F32))))

    operands = [
        x,
        jnp.asarray(_bucket_table()),
        rel_bias.astype(F32),
        attn_sinks[layer].astype(F32),
        qk_bound.reshape(1),
        jnp.asarray(_head_mean_matrix(), dtype=BF16),
        row(norm_mix_g[layer]),
        q_gain.reshape(N_COL_CHUNKS, 1, COL_CHUNK),
        row(jnp.tile(k_norm_g[layer], N_KV_HEADS)),
        col_chunks(w[:, :q_end]), w[:, q_end:v_end], w[:, v_end:a_end], w[:, a_end:g_end],
        jnp.concatenate([col_chunks(w[:, g_end:ga_end]), col_chunks(w[:, ga_end:])], axis=2),
        w_attn_o[layer].astype(BF16),
        lane_blocks(w_dw[layer]),
        lane_blocks(b_dw[layer].reshape(1, -1)),
        row(conv_ln_g[layer]), row(conv_ln_b[layer]),
        w_conv_out[layer].astype(BF16),
        w_out[layer].astype(BF16),
    ]
    in_specs = [pl.BlockSpec((1, MIX_TILE, D_MODEL), lambda b, j: (b, j, 0)),
                _resident(operands[1].shape), _smem(), _smem(), _smem()]
    in_specs += [_resident(op.shape) for op in operands[5:]]

    x1 = pl.pallas_call(
        _mixer_kernel,
        grid=(batch, seq // MIX_TILE),
        in_specs=in_specs,
        out_specs=pl.BlockSpec((1, MIX_TILE, D_MODEL), lambda b, j: (b, j, 0)),
        out_shape=jax.ShapeDtypeStruct(x.shape, F32),
        scratch_shapes=[
            pltpu.VMEM((2, N_Q_HEADS, BLOCK, 2 * BLOCK), F32),
            pltpu.SMEM((N_Q_HEADS,), F32),
            pltpu.SMEM((1,), jnp.int32),
            pltpu.VMEM((MIX_TILE, D_MODEL), BF16),
            pltpu.VMEM((2, N_COL_CHUNKS, MIX_TILE, COL_CHUNK), BF16),
            pltpu.VMEM((N_KV_HEADS, BLOCK + MIX_TILE, LANES), BF16),
            pltpu.VMEM((N_KV_HEADS, BLOCK + MIX_TILE, LANES), BF16),
            pltpu.VMEM((MIX_TILE, D_MODEL), BF16),
            pltpu.VMEM((N_LANE_BLOCKS, CONV_HALO + MIX_TILE, LANES), F32),
            pltpu.VMEM((N_LANE_BLOCKS, MIX_TILE, LANES), F32),
            pltpu.VMEM((N_COL_CHUNKS, MIX_TILE, COL_CHUNK), BF16),
            pltpu.VMEM((N_COL_CHUNKS, MIX_TILE, COL_CHUNK), BF16),
        ],
        compiler_params=pltpu.CompilerParams(
            dimension_semantics=("arbitrary", "arbitrary"), vmem_limit_bytes=VMEM_LIMIT),
        name="mixer",
    )(*operands)

    tokens = batch * seq
    out = pl.pallas_call(
        _mlp_kernel,
        grid=(tokens // MLP_TILE,),
        in_specs=[pl.BlockSpec((MLP_TILE, D_MODEL), lambda i: (i, 0)),
                  _resident((1, D_MODEL)), _resident((D_MODEL, D_FF)), _resident((D_FF, D_MODEL))],
        out_specs=pl.BlockSpec((MLP_TILE, D_MODEL), lambda i: (i, 0)),
        out_shape=jax.ShapeDtypeStruct((tokens, D_MODEL), F32),
        scratch_shapes=[pltpu.VMEM((MLP_TILE, D_MODEL), BF16)],
        compiler_params=pltpu.CompilerParams(
            dimension_semantics=("arbitrary",), vmem_limit_bytes=VMEM_LIMIT),
        name="mlp",
    )(x1.reshape(tokens, D_MODEL), row(norm_mlp_g[layer]),
      w_ff1[layer].astype(BF16), w_ff2[layer].astype(BF16))
    return out.reshape(batch, seq, D_MODEL)
```

```python
import functools

import numpy as np
import jax
import jax.numpy as jnp
from jax import lax
from jax.experimental import pallas as pl
from jax.experimental.pallas import tpu as pltpu

D_MODEL = 1024
HEAD_DIM = 64
N_Q_HEADS = 16
N_KV_HEADS = 4
KV_WIDTH = N_KV_HEADS * HEAD_DIM
GROUP = N_Q_HEADS // N_KV_HEADS
BLOCK = 128
CONV_WIDTH = 31
D_FF = 4 * D_MODEL
N_BUCKETS = 32
MAX_DISTANCE = 128
EPS = 1e-6
NEG = -1e30
LOG2E = 1.4426950408889634
QK_BOUND_MARGIN = 1.03
MAX_EXP2_SPREAD = 100.0

LANES = 128
SUBLANES = 8
N_LANE_BLOCKS = D_MODEL // LANES
COL_CHUNK = 2 * LANES
N_COL_CHUNKS = D_MODEL // COL_CHUNK
CONV_HALO = 32
CONV_SKIP = CONV_HALO - (CONV_WIDTH - 1)
CONV_ROWS = 64

MIX_TILE = 512
MLP_TILE = 1024
FF_CHUNK = 2048
VMEM_LIMIT = 56 * 1024 * 1024

F32 = jnp.float32
BF16 = jnp.bfloat16


def _bucket_table():
    qi = np.arange(BLOCK, dtype=np.int32)[:, None]
    kj = np.arange(2 * BLOCK, dtype=np.int32)[None, :]
    dist = qi + BLOCK - kj
    n = np.maximum(dist, 0)
    max_exact = N_BUCKETS // 2
    nf = np.maximum(n, 1).astype(np.float32)
    large = max_exact + (np.log(nf / np.float32(max_exact)) / np.float32(np.log(MAX_DISTANCE / max_exact))
                         * np.float32(N_BUCKETS - max_exact)).astype(np.int32)
    large = np.minimum(large, N_BUCKETS - 1)
    bucket = np.where(n < max_exact, n, large)
    valid = (dist >= 0) & (dist < BLOCK)
    return np.where(valid, bucket, -1).astype(np.int32)


def _head_mean_matrix():
    idx = np.arange(COL_CHUNK) // HEAD_DIM
    return (idx[:, None] == idx[None, :]).astype(np.float32) / HEAD_DIM


def _dot(a, b):
    return jnp.dot(a, b, preferred_element_type=F32)


def _dot_nt(a, b):
    return lax.dot_general(a, b, (((1,), (1,)), ((), ())), preferred_element_type=F32)


def _rms(x, gain):
    return x * lax.rsqrt(jnp.mean(x * x, axis=-1, keepdims=True) + EPS) * gain


def _head_rms(t, hm, gain):
    ms = _dot((t * t).astype(BF16), hm)
    return t * lax.rsqrt(ms + EPS) * gain


def _mixer_kernel(x_ref, bucket_ref, relb_ref, sink_ref, qk_bound_ref, hm_ref,
                  g_mix_ref, gq_ref, gk_ref, w_q_ref, w_kv_ref, w_a_ref, w_g_ref,
                  w_gates_ref, w_ao_ref, w_dw_ref, b_dw_ref, ln_g_ref, ln_b_ref,
                  w_co_ref, w_out_ref,
                  o_ref,
                  bias_s, sink_s, bounded_s, u_s, q_s, kdup_s, vdup_s, attn_s, h_s, conv_s, ga_s, gc_s):
    tile = x_ref.shape[1]
    n_blocks = tile // BLOCK
    b = pl.program_id(0)
    j = pl.program_id(1)

    @pl.when((b == 0) & (j == 0))
    def _build_bias():
        qk_bound = qk_bound_ref[0]
        his = [relb_ref[0, h] for h in range(N_Q_HEADS)]
        los = list(his)
        for bb in range(1, N_BUCKETS):
            his = [jnp.maximum(hi, relb_ref[bb, h]) for h, hi in enumerate(his)]
            los = [jnp.minimum(lo, relb_ref[bb, h]) for h, lo in enumerate(los)]
        spread = (his[0] - los[0]) * LOG2E
        for h in range(1, N_Q_HEADS):
            spread = jnp.maximum(spread, (his[h] - los[h]) * LOG2E)
        bounded = 2.0 * qk_bound + spread <= MAX_EXP2_SPREAD
        bounded_s[0] = bounded.astype(jnp.int32)

        bk = bucket_ref[...]
        left = lax.broadcasted_iota(jnp.int32, bk.shape, 1) < BLOCK
        for h in range(N_Q_HEADS):
            shift = jnp.where(bounded, qk_bound + his[h] * LOG2E, 0.0)
            acc = jnp.full(bk.shape, NEG, F32)
            for bb in range(N_BUCKETS):
                acc = jnp.where(bk == bb, relb_ref[bb, h] * LOG2E - shift, acc)
            bias_s[0, h] = acc
            bias_s[1, h] = jnp.where(left, NEG, acc)
            sink_s[h] = jnp.max(jnp.exp2(jnp.full((SUBLANES, LANES), sink_ref[h] * LOG2E - shift, F32)))

    @pl.when(j == 0)
    def _zero_halo():
        kdup_s[:, 0:BLOCK, :] = jnp.zeros((N_KV_HEADS, BLOCK, LANES), BF16)
        vdup_s[:, 0:BLOCK, :] = jnp.zeros((N_KV_HEADS, BLOCK, LANES), BF16)
        h_s[:, 0:CONV_HALO, :] = jnp.zeros((N_LANE_BLOCKS, CONV_HALO, LANES), F32)

    @pl.when(j != 0)
    def _carry_halo():
        kdup_s[:, 0:BLOCK, :] = kdup_s[:, tile:tile + BLOCK, :]
        vdup_s[:, 0:BLOCK, :] = vdup_s[:, tile:tile + BLOCK, :]
        h_s[:, 0:CONV_HALO, :] = h_s[:, tile:tile + CONV_HALO, :]

    x = x_ref[0]
    u = _rms(x, g_mix_ref[...]).astype(BF16)
    hm = hm_ref[...]

    def _conv_lane_block(c):
        for r in range(tile // CONV_ROWS):
            acc = jnp.broadcast_to(b_dw_ref[c], (CONV_ROWS, LANES))
            for a in range(SUBLANES):
                taps = range(a, CONV_WIDTH, SUBLANES)
                start = r * CONV_ROWS + CONV_SKIP + a
                rows = h_s[c, start:start + CONV_ROWS + SUBLANES * (len(taps) - 1), :]
                for k, t in enumerate(taps):
                    acc = acc + rows[SUBLANES * k:SUBLANES * k + CONV_ROWS] * w_dw_ref[c, t:t + 1, :]
            conv_s[c, r * CONV_ROWS:(r + 1) * CONV_ROWS, :] = acc

    def _gate_chunk(i, rows):
        gates = _dot(rows, w_gates_ref[i]).astype(BF16)
        ga_s[i] = gates[:, :COL_CHUNK]
        gc_s[i] = gates[:, COL_CHUNK:]

    hglu = _dot(u, w_a_ref[...]) * jax.nn.sigmoid(_dot(u, w_g_ref[...]))
    for c in range(N_LANE_BLOCKS):
        h_s[c, CONV_HALO:CONV_HALO + tile, :] = hglu[:, c * LANES:(c + 1) * LANES]

    kv = _dot(u, w_kv_ref[...])
    kn = _head_rms(kv[:, :KV_WIDTH], hm, gk_ref[...])
    vv = kv[:, KV_WIDTH:]
    low = lax.broadcasted_iota(jnp.int32, (tile, LANES), 1) < HEAD_DIM
    for src, dst in ((kn, kdup_s), (vv, vdup_s)):
        for c in range(KV_WIDTH // LANES):
            s0 = src[:, c * LANES:(c + 1) * LANES]
            s1 = pltpu.roll(s0, HEAD_DIM, 1)
            dst[2 * c + 0, BLOCK:BLOCK + tile, :] = jnp.where(low, s0, s1).astype(BF16)
            dst[2 * c + 1, BLOCK:BLOCK + tile, :] = jnp.where(low, s1, s0).astype(BF16)
    u_s[...] = u

    def _conv_and_q(i, carry):
        _conv_lane_block(i)
        qn = _head_rms(_dot(u_s[...], w_q_ref[i]), hm, gq_ref[i])
        even = (lax.broadcasted_iota(jnp.int32, qn.shape, 1) & HEAD_DIM) == 0
        q_s[0, i] = jnp.where(even, qn, 0.0).astype(BF16)
        q_s[1, i] = jnp.where(even, 0.0, qn).astype(BF16)
        return carry

    def _conv_and_gates(i, carry):
        _conv_lane_block(i + N_COL_CHUNKS)
        _gate_chunk(i, u_s[...])
        return carry

    lax.fori_loop(0, N_COL_CHUNKS, _conv_and_q, 0)
    lax.fori_loop(0, N_COL_CHUNKS, _conv_and_gates, 0)

    def _attn_block(bounded, n, carry):
        r0 = pl.multiple_of(n * BLOCK, BLOCK)
        first = jnp.where((j == 0) & (n == 0), 1, 0)
        low_b = lax.broadcasted_iota(jnp.int32, (BLOCK, LANES), 1) < HEAD_DIM
        for g in range(N_KV_HEADS):
            q4 = jnp.concatenate(
                [q_s[i, g, pl.ds(r0, BLOCK), pr * LANES:(pr + 1) * LANES] for pr in range(2) for i in range(2)],
                axis=0)
            kb = kdup_s[g, pl.ds(r0, 2 * BLOCK), :]
            vb = vdup_s[g, pl.ds(r0, 2 * BLOCK), :]
            s4 = _dot_nt(q4, kb)
            es, rs = [], []
            for hh in range(GROUP):
                h = GROUP * g + hh
                s = s4[hh * BLOCK:(hh + 1) * BLOCK] + bias_s[first, h]
                if bounded:
                    e = jnp.exp2(s)
                    denom = jnp.sum(e, axis=-1, keepdims=True) + sink_s[h]
                else:
                    m = jnp.max(s, axis=-1, keepdims=True)
                    e = jnp.exp2(s - m)
                    denom = jnp.sum(e, axis=-1, keepdims=True) + jnp.exp2(sink_ref[h] * LOG2E - m)
                es.append(e.astype(BF16))
                rs.append(1.0 / denom)
            o4 = _dot(jnp.concatenate(es, axis=0), vb)
            for pr in range(2):
                o_even = o4[(2 * pr) * BLOCK:(2 * pr + 1) * BLOCK] * rs[2 * pr]
                o_odd = o4[(2 * pr + 1) * BLOCK:(2 * pr + 2) * BLOCK] * rs[2 * pr + 1]
                p = 2 * g + pr
                attn_s[pl.ds(r0, BLOCK), p * LANES:(p + 1) * LANES] = jnp.where(low_b, o_even, o_odd).astype(BF16)
        return carry

    @pl.when(bounded_s[0] == 1)
    def _attention_constant_shift():
        lax.fori_loop(0, n_blocks, functools.partial(_attn_block, True), 0, unroll=True)

    @pl.when(bounded_s[0] != 1)
    def _attention_row_max():
        lax.fori_loop(0, n_blocks, functools.partial(_attn_block, False), 0)

    gate_a = jax.nn.sigmoid(jnp.concatenate([ga_s[i] for i in range(N_COL_CHUNKS)], axis=1).astype(F32))
    merged = gate_a * _dot(attn_s[...], w_ao_ref[...])

    cv = jnp.concatenate([conv_s[c] for c in range(N_LANE_BLOCKS)], axis=1)
    mu = jnp.mean(cv, axis=-1, keepdims=True)
    xc = cv - mu
    var = jnp.mean(xc * xc, axis=-1, keepdims=True)
    y = xc * lax.rsqrt(var + EPS) * ln_g_ref[...] + ln_b_ref[...]
    conv = _dot((y * jax.nn.sigmoid(y)).astype(BF16), w_co_ref[...])

    gate_c = jax.nn.sigmoid(jnp.concatenate([gc_s[i] for i in range(N_COL_CHUNKS)], axis=1).astype(F32))
    merged = merged + gate_c * conv
    o_ref[0] = x + _dot(merged.astype(BF16), w_out_ref[...])


def _mlp_kernel(x_ref, g_ref, w1_ref, w2_ref, o_ref, xn_s):
    x = x_ref[...]
    xn_s[...] = _rms(x, g_ref[...]).astype(BF16)
    acc = x
    for c in range(D_FF // FF_CHUNK):
        cols = slice(c * FF_CHUNK, (c + 1) * FF_CHUNK)
        hmid = jnp.square(jnp.maximum(_dot(xn_s[...], w1_ref[:, cols]), 0.0))
        acc = acc + _dot(hmid.astype(BF16), w2_ref[cols, :])
    o_ref[...] = acc


def _resident(shape):
    zeros = (0,) * len(shape)
    return pl.BlockSpec(shape, lambda *_: zeros, pipeline_mode=pl.Buffered(1))


def _smem():
    return pl.BlockSpec(memory_space=pltpu.SMEM)


def kernel(x, norm_mix_g, w_in, q_norm_g, k_norm_g, attn_sinks, rel_bias, w_attn_o, w_dw, b_dw,
           conv_ln_g, conv_ln_b, w_conv_out, w_out, norm_mlp_g, w_ff1, w_ff2):
    batch, seq, d = x.shape
    assert d == D_MODEL and seq % MIX_TILE == 0 and (batch * seq) % MLP_TILE == 0
    layer = 0
    row = lambda v: v.reshape(1, -1).astype(F32)
    w = w_in[layer].astype(BF16)
    q_end = N_Q_HEADS * HEAD_DIM
    k_end = q_end + KV_WIDTH
    v_end = k_end + KV_WIDTH
    a_end = v_end + D_MODEL
    g_end = a_end + D_MODEL
    ga_end = g_end + D_MODEL
    lane_blocks = lambda v: v.astype(F32).reshape(-1, N_LANE_BLOCKS, LANES).transpose(1, 0, 2)
    col_chunks = lambda m: m.reshape(m.shape[0], N_COL_CHUNKS, COL_CHUNK).transpose(1, 0, 2)
    q_gain = jnp.tile(q_norm_g[layer].astype(F32), N_Q_HEADS) * (HEAD_DIM ** -0.5 * LOG2E)
    qk_bound = (QK_BOUND_MARGIN * HEAD_DIM * jnp.max(jnp.abs(q_gain))
                * jnp.max(jnp.abs(k_norm_g[layer].astype(F32))))

    operands = [
        x,
        jnp.asarray(_bucket_table()),
        rel_bias.astype(F32),
        attn_sinks[layer].astype(F32),
        qk_bound.reshape(1),
        jnp.asarray(_head_mean_matrix(), dtype=BF16),
        row(norm_mix_g[layer]),
        q_gain.reshape(N_COL_CHUNKS, 1, COL_CHUNK),
        row(jnp.tile(k_norm_g[layer], N_KV_HEADS)),
        col_chunks(w[:, :q_end]), w[:, q_end:v_end], w[:, v_end:a_end], w[:, a_end:g_end],
        jnp.concatenate([col_chunks(w[:, g_end:ga_end]), col_chunks(w[:, ga_end:])], axis=2),
        w_attn_o[layer].astype(BF16),
        lane_blocks(w_dw[layer]),
        lane_blocks(b_dw[layer].reshape(1, -1)),
        row(conv_ln_g[layer]), row(conv_ln_b[layer]),
        w_conv_out[layer].astype(BF16),
        w_out[layer].astype(BF16),
    ]
    in_specs = [pl.BlockSpec((1, MIX_TILE, D_MODEL), lambda b, j: (b, j, 0)),
                _resident(operands[1].shape), _smem(), _smem(), _smem()]
    in_specs += [_resident(op.shape) for op in operands[5:]]

    x1 = pl.pallas_call(
        _mixer_kernel,
        grid=(batch, seq // MIX_TILE),
        in_specs=in_specs,
        out_specs=pl.BlockSpec((1, MIX_TILE, D_MODEL), lambda b, j: (b, j, 0)),
        out_shape=jax.ShapeDtypeStruct(x.shape, F32),
        scratch_shapes=[
            pltpu.VMEM((2, N_Q_HEADS, BLOCK, 2 * BLOCK), F32),
            pltpu.SMEM((N_Q_HEADS,), F32),
            pltpu.SMEM((1,), jnp.int32),
            pltpu.VMEM((MIX_TILE, D_MODEL), BF16),
            pltpu.VMEM((2, N_COL_CHUNKS, MIX_TILE, COL_CHUNK), BF16),
            pltpu.VMEM((N_KV_HEADS, BLOCK + MIX_TILE, LANES), BF16),
            pltpu.VMEM((N_KV_HEADS, BLOCK + MIX_TILE, LANES), BF16),
            pltpu.VMEM((MIX_TILE, D_MODEL), BF16),
            pltpu.VMEM((N_LANE_BLOCKS, CONV_HALO + MIX_TILE, LANES), F32),
            pltpu.VMEM((N_LANE_BLOCKS, MIX_TILE, LANES), F32),
            pltpu.VMEM((N_COL_CHUNKS, MIX_TILE, COL_CHUNK), BF16),
            pltpu.VMEM((N_COL_CHUNKS, MIX_TILE, COL_CHUNK), BF16),
        ],
        compiler_params=pltpu.CompilerParams(
            dimension_semantics=("arbitrary", "arbitrary"), vmem_limit_bytes=VMEM_LIMIT),
        name="mixer",
    )(*operands)

    tokens = batch * seq
    out = pl.pallas_call(
        _mlp_kernel,
        grid=(tokens // MLP_TILE,),
        in_specs=[pl.BlockSpec((MLP_TILE, D_MODEL), lambda i: (i, 0)),
                  _resident((1, D_MODEL)), _resident((D_MODEL, D_FF)), _resident((D_FF, D_MODEL))],
        out_specs=pl.BlockSpec((MLP_TILE, D_MODEL), lambda i: (i, 0)),
        out_shape=jax.ShapeDtypeStruct((tokens, D_MODEL), F32),
        scratch_shapes=[pltpu.VMEM((MLP_TILE, D_MODEL), BF16)],
        compiler_params=pltpu.CompilerParams(
            dimension_semantics=("arbitrary",), vmem_limit_bytes=VMEM_LIMIT),
        name="mlp",
    )(x1.reshape(tokens, D_MODEL), row(norm_mlp_g[layer]),
      w_ff1[layer].astype(BF16), w_ff2[layer].astype(BF16))
    return out.reshape(batch, seq, D_MODEL)
```

```python
import functools

import numpy as np
import jax
import jax.numpy as jnp
from jax import lax
from jax.experimental import pallas as pl
from jax.experimental.pallas import tpu as pltpu

D_MODEL = 1024
HEAD_DIM = 64
N_Q_HEADS = 16
N_KV_HEADS = 4
KV_WIDTH = N_KV_HEADS * HEAD_DIM
GROUP = N_Q_HEADS // N_KV_HEADS
BLOCK = 128
CONV_WIDTH = 31
D_FF = 4 * D_MODEL
N_BUCKETS = 32
MAX_DISTANCE = 128
EPS = 1e-6
NEG = -1e30
LOG2E = 1.4426950408889634
QK_BOUND_MARGIN = 1.03
MAX_EXP2_SPREAD = 100.0

LANES = 128
SUBLANES = 8
N_LANE_BLOCKS = D_MODEL // LANES
COL_CHUNK = 2 * LANES
N_COL_CHUNKS = D_MODEL // COL_CHUNK
CONV_HALO = 32
CONV_SKIP = CONV_HALO - (CONV_WIDTH - 1)
CONV_ROWS = 64

MIX_TILE = 512
MLP_TILE = 1024
FF_CHUNK = 2048
VMEM_LIMIT = 56 * 1024 * 1024

F32 = jnp.float32
BF16 = jnp.bfloat16


def _bucket_table():
    qi = np.arange(BLOCK, dtype=np.int32)[:, None]
    kj = np.arange(2 * BLOCK, dtype=np.int32)[None, :]
    dist = qi + BLOCK - kj
    n = np.maximum(dist, 0)
    max_exact = N_BUCKETS // 2
    nf = np.maximum(n, 1).astype(np.float32)
    large = max_exact + (np.log(nf / np.float32(max_exact)) / np.float32(np.log(MAX_DISTANCE / max_exact))
                         * np.float32(N_BUCKETS - max_exact)).astype(np.int32)
    large = np.minimum(large, N_BUCKETS - 1)
    bucket = np.where(n < max_exact, n, large)
    valid = (dist >= 0) & (dist < BLOCK)
    return np.where(valid, bucket, -1).astype(np.int32)


def _head_mean_matrix():
    idx = np.arange(COL_CHUNK) // HEAD_DIM
    return (idx[:, None] == idx[None, :]).astype(np.float32) / HEAD_DIM


def _dot(a, b):
    return jnp.dot(a, b, preferred_element_type=F32)


def _dot_nt(a, b):
    return lax.dot_general(a, b, (((1,), (1,)), ((), ())), preferred_element_type=F32)


def _rms(x, gain):
    return x * lax.rsqrt(jnp.mean(x * x, axis=-1, keepdims=True) + EPS) * gain


def _head_rms(t, hm, gain):
    ms = _dot((t * t).astype(BF16), hm)
    return t * lax.rsqrt(ms + EPS) * gain


def _mixer_kernel(x_ref, bucket_ref, relb_ref, sink_ref, qk_bound_ref, hm_ref,
                  g_mix_ref, gq_ref, gk_ref, w_q_ref, w_kv_ref, w_a_ref, w_g_ref,
                  w_gates_ref, w_ao_ref, w_dw_ref, b_dw_ref, ln_g_ref, ln_b_ref,
                  w_co_ref, w_out_ref,
                  o_ref,
                  bias_s, sink_s, bounded_s, u_s, q_s, kdup_s, vdup_s, attn_s, h_s, conv_s, ga_s, gc_s):
    tile = x_ref.shape[1]
    n_blocks = tile // BLOCK
    b = pl.program_id(0)
    j = pl.program_id(1)

    @pl.when((b == 0) & (j == 0))
    def _build_bias():
        qk_bound = qk_bound_ref[0]
        his = [relb_ref[0, h] for h in range(N_Q_HEADS)]
        los = list(his)
        for bb in range(1, N_BUCKETS):
            his = [jnp.maximum(hi, relb_ref[bb, h]) for h, hi in enumerate(his)]
            los = [jnp.minimum(lo, relb_ref[bb, h]) for h, lo in enumerate(los)]
        spread = (his[0] - los[0]) * LOG2E
        for h in range(1, N_Q_HEADS):
            spread = jnp.maximum(spread, (his[h] - los[h]) * LOG2E)
        bounded = 2.0 * qk_bound + spread <= MAX_EXP2_SPREAD
        bounded_s[0] = bounded.astype(jnp.int32)

        bk = bucket_ref[...]
        left = lax.broadcasted_iota(jnp.int32, bk.shape, 1) < BLOCK
        for h in range(N_Q_HEADS):
            shift = jnp.where(bounded, qk_bound + his[h] * LOG2E, 0.0)
            acc = jnp.full(bk.shape, NEG, F32)
            for bb in range(N_BUCKETS):
                acc = jnp.where(bk == bb, relb_ref[bb, h] * LOG2E - shift, acc)
            bias_s[0, h] = acc
            bias_s[1, h] = jnp.where(left, NEG, acc)
            sink_s[h] = jnp.max(jnp.exp2(jnp.full((SUBLANES, LANES), sink_ref[h] * LOG2E - shift, F32)))

    @pl.when(j == 0)
    def _zero_halo():
        kdup_s[:, 0:BLOCK, :] = jnp.zeros((N_KV_HEADS, BLOCK, LANES), BF16)
        vdup_s[:, 0:BLOCK, :] = jnp.zeros((N_KV_HEADS, BLOCK, LANES), BF16)
        h_s[:, 0:CONV_HALO, :] = jnp.zeros((N_LANE_BLOCKS, CONV_HALO, LANES), F32)

    @pl.when(j != 0)
    def _carry_halo():
        kdup_s[:, 0:BLOCK, :] = kdup_s[:, tile:tile + BLOCK, :]
        vdup_s[:, 0:BLOCK, :] = vdup_s[:, tile:tile + BLOCK, :]
        h_s[:, 0:CONV_HALO, :] = h_s[:, tile:tile + CONV_HALO, :]

    x = x_ref[0]
    u = _rms(x, g_mix_ref[...]).astype(BF16)
    hm = hm_ref[...]

    def _conv_lane_block(c):
        for r in range(tile // CONV_ROWS):
            acc = jnp.broadcast_to(b_dw_ref[c], (CONV_ROWS, LANES))
            for a in range(SUBLANES):
                taps = range(a, CONV_WIDTH, SUBLANES)
                start = r * CONV_ROWS + CONV_SKIP + a
                rows = h_s[c, start:start + CONV_ROWS + SUBLANES * (len(taps) - 1), :]
                for k, t in enumerate(taps):
                    acc = acc + rows[SUBLANES * k:SUBLANES * k + CONV_ROWS] * w_dw_ref[c, t:t + 1, :]
            conv_s[c, r * CONV_ROWS:(r + 1) * CONV_ROWS, :] = acc

    def _gate_chunk(i, rows):
        gates = _dot(rows, w_gates_ref[i]).astype(BF16)
        ga_s[i] = gates[:, :COL_CHUNK]
        gc_s[i] = gates[:, COL_CHUNK:]

    hglu = _dot(u, w_a_ref[...]) * jax.nn.sigmoid(_dot(u, w_g_ref[...]))
    for c in range(N_LANE_BLOCKS):
        h_s[c, CONV_HALO:CONV_HALO + tile, :] = hglu[:, c * LANES:(c + 1) * LANES]

    kv = _dot(u, w_kv_ref[...])
    kn = _head_rms(kv[:, :KV_WIDTH], hm, gk_ref[...])
    vv = kv[:, KV_WIDTH:]
    low = lax.broadcasted_iota(jnp.int32, (tile, LANES), 1) < HEAD_DIM
    for src, dst in ((kn, kdup_s), (vv, vdup_s)):
        for c in range(KV_WIDTH // LANES):
            s0 = src[:, c * LANES:(c + 1) * LANES]
            s1 = pltpu.roll(s0, HEAD_DIM, 1)
            dst[2 * c + 0, BLOCK:BLOCK + tile, :] = jnp.where(low, s0, s1).astype(BF16)
            dst[2 * c + 1, BLOCK:BLOCK + tile, :] = jnp.where(low, s1, s0).astype(BF16)
    u_s[...] = u

    def _conv_and_q(i, carry):
        _conv_lane_block(i)
        qn = _head_rms(_dot(u_s[...], w_q_ref[i]), hm, gq_ref[i])
        even = (lax.broadcasted_iota(jnp.int32, qn.shape, 1) & HEAD_DIM) == 0
        q_s[0, i] = jnp.where(even, qn, 0.0).astype(BF16)
        q_s[1, i] = jnp.where(even, 0.0, qn).astype(BF16)
        return carry

    def _conv_and_gates(i, carry):
        _conv_lane_block(i + N_COL_CHUNKS)
        _gate_chunk(i, u_s[...])
        return carry

    lax.fori_loop(0, N_COL_CHUNKS, _conv_and_q, 0)
    lax.fori_loop(0, N_COL_CHUNKS, _conv_and_gates, 0)

    def _attn_block(bounded, n, carry):
        r0 = pl.multiple_of(n * BLOCK, BLOCK)
        first = jnp.where((j == 0) & (n == 0), 1, 0)
        low_b = lax.broadcasted_iota(jnp.int32, (BLOCK, LANES), 1) < HEAD_DIM
        for g in range(N_KV_HEADS):
            q4 = jnp.concatenate(
                [q_s[i, g, pl.ds(r0, BLOCK), pr * LANES:(pr + 1) * LANES] for pr in range(2) for i in range(2)],
                axis=0)
            kb = kdup_s[g, pl.ds(r0, 2 * BLOCK), :]
            vb = vdup_s[g, pl.ds(r0, 2 * BLOCK), :]
            s4 = _dot_nt(q4, kb)
            es, rs = [], []
            for hh in range(GROUP):
                h = GROUP * g + hh
                s = s4[hh * BLOCK:(hh + 1) * BLOCK] + bias_s[first, h]
                if bounded:
                    e = jnp.exp2(s)
                    denom = jnp.sum(e, axis=-1, keepdims=True) + sink_s[h]
                else:
                    m = jnp.max(s, axis=-1, keepdims=True)
                    e = jnp.exp2(s - m)
                    denom = jnp.sum(e, axis=-1, keepdims=True) + jnp.exp2(sink_ref[h] * LOG2E - m)
                es.append(e.astype(BF16))
                rs.append(1.0 / denom)
            o4 = _dot(jnp.concatenate(es, axis=0), vb)
            for pr in range(2):
                o_even = o4[(2 * pr) * BLOCK:(2 * pr + 1) * BLOCK] * rs[2 * pr]
                o_odd = o4[(2 * pr + 1) * BLOCK:(2 * pr + 2) * BLOCK] * rs[2 * pr + 1]
                p = 2 * g + pr
                attn_s[pl.ds(r0, BLOCK), p * LANES:(p + 1) * LANES] = jnp.where(low_b, o_even, o_odd).astype(BF16)
        return carry

    @pl.when(bounded_s[0] == 1)
    def _attention_constant_shift():
        lax.fori_loop(0, n_blocks, functools.partial(_attn_block, True), 0, unroll=True)

    @pl.when(bounded_s[0] != 1)
    def _attention_row_max():
        lax.fori_loop(0, n_blocks, functools.partial(_attn_block, False), 0)

    gate_a = jax.nn.sigmoid(jnp.concatenate([ga_s[i] for i in range(N_COL_CHUNKS)], axis=1).astype(F32))
    merged = gate_a * _dot(attn_s[...], w_ao_ref[...])

    cv = jnp.concatenate([conv_s[c] for c in range(N_LANE_BLOCKS)], axis=1)
    mu = jnp.mean(cv, axis=-1, keepdims=True)
    xc = cv - mu
    var = jnp.mean(xc * xc, axis=-1, keepdims=True)
    y = xc * lax.rsqrt(var + EPS) * ln_g_ref[...] + ln_b_ref[...]
    conv = _dot((y * jax.nn.sigmoid(y)).astype(BF16), w_co_ref[...])

    gate_c = jax.nn.sigmoid(jnp.concatenate([gc_s[i] for i in range(N_COL_CHUNKS)], axis=1).astype(F32))
    merged = merged + gate_c * conv
    o_ref[0] = x + _dot(merged.astype(BF16), w_out_ref[...])


def _mlp_kernel(x_ref, g_ref, w1_ref, w2_ref, o_ref, xn_s):
    x = x_ref[...]
    xn_s[...] = _rms(x, g_ref[...]).astype(BF16)
    acc = x
    for c in range(D_FF // FF_CHUNK):
        cols = slice(c * FF_CHUNK, (c + 1) * FF_CHUNK)
        hmid = jnp.square(jnp.maximum(_dot(xn_s[...], w1_ref[:, cols]), 0.0))
        acc = acc + _dot(hmid.astype(BF16), w2_ref[cols, :])
    o_ref[...] = acc


def _resident(shape):
    zeros = (0,) * len(shape)
    return pl.BlockSpec(shape, lambda *_: zeros, pipeline_mode=pl.Buffered(1))


def _smem():
    return pl.BlockSpec(memory_space=pltpu.SMEM)


def kernel(x, norm_mix_g, w_in, q_norm_g, k_norm_g, attn_sinks, rel_bias, w_attn_o, w_dw, b_dw,
           conv_ln_g, conv_ln_b, w_conv_out, w_out, norm_mlp_g, w_ff1, w_ff2):
    batch, seq, d = x.shape
    assert d == D_MODEL and seq % MIX_TILE == 0 and (batch * seq) % MLP_TILE == 0
    assert MIX_TILE & (MIX_TILE - 1) == 0
    layer = 0
    row = lambda v: v.reshape(1, -1).astype(F32)
    w = w_in[layer].astype(BF16)
    q_end = N_Q_HEADS * HEAD_DIM
    k_end = q_end + KV_WIDTH
    v_end = k_end + KV_WIDTH
    a_end = v_end + D_MODEL
    g_end = a_end + D_MODEL
    ga_end = g_end + D_MODEL
    lane_blocks = lambda v: v.astype(F32).reshape(-1, N_LANE_BLOCKS, LANES).transpose(1, 0, 2)
    col_chunks = lambda m: m.reshape(m.shape[0], N_COL_CHUNKS, COL_CHUNK).transpose(1, 0, 2)
    q_gain = jnp.tile(q_norm_g[layer].astype(F32), N_Q_HEADS) * (HEAD_DIM ** -0.5 * LOG2E)
    qk_bound = (QK_BOUND_MARGIN * HEAD_DIM * jnp.max(jnp.abs(q_gain))
                * jnp.max(jnp.abs(k_norm_g[layer].astype(F32))))

    operands = [
        x,
        jnp.asarray(_bucket_table()),
        rel_bias.astype(F32),
        attn_sinks[layer].astype(F32),
        qk_bound.reshape(1),
        jnp.asarray(_head_mean_matrix(), dtype=BF16),
        row(norm_mix_g[layer]),
        q_gain.reshape(N_COL_CHUNKS, 1, COL_CHUNK),
        row(jnp.tile(k_norm_g[layer], N_KV_HEADS)),
        col_chunks(w[:, :q_end]), w[:, q_end:v_end], w[:, v_end:a_end], w[:, a_end:g_end],
        jnp.concatenate([col_chunks(w[:, g_end:ga_end]), col_chunks(w[:, ga_end:])], axis=2),
        w_attn_o[layer].astype(BF16),
        lane_blocks(w_dw[layer]),
        lane_blocks(b_dw[layer].reshape(1, -1)),
        row(conv_ln_g[layer]), row(conv_ln_b[layer]),
        w_conv_out[layer].astype(BF16),
        w_out[layer].astype(BF16),
    ]
    in_specs = [pl.BlockSpec((1, MIX_TILE, D_MODEL), lambda b, j: (b, j, 0)),
                _resident(operands[1].shape), _smem(), _smem(), _smem()]
    in_specs += [_resident(op.shape) for op in operands[5:]]

    x1 = pl.pallas_call(
        _mixer_kernel,
        grid=(batch, seq // MIX_TILE),
        in_specs=in_specs,
        out_specs=pl.BlockSpec((1, MIX_TILE, D_MODEL), lambda b, j: (b, j, 0)),
        out_shape=jax.ShapeDtypeStruct(x.shape, F32),
        scratch_shapes=[
            pltpu.VMEM((2, N_Q_HEADS, BLOCK, 2 * BLOCK), F32),
            pltpu.SMEM((N_Q_HEADS,), F32),
            pltpu.SMEM((1,), jnp.int32),
            pltpu.VMEM((MIX_TILE, D_MODEL), BF16),
            pltpu.VMEM((2, N_COL_CHUNKS, MIX_TILE, COL_CHUNK), BF16),
            pltpu.VMEM((N_KV_HEADS, BLOCK + MIX_TILE, LANES), BF16),
            pltpu.VMEM((N_KV_HEADS, BLOCK + MIX_TILE, LANES), BF16),
            pltpu.VMEM((MIX_TILE, D_MODEL), BF16),
            pltpu.VMEM((N_LANE_BLOCKS, CONV_HALO + MIX_TILE, LANES), F32),
            pltpu.VMEM((N_LANE_BLOCKS, MIX_TILE, LANES), F32),
            pltpu.VMEM((N_COL_CHUNKS, MIX_TILE, COL_CHUNK), BF16),
            pltpu.VMEM((N_COL_CHUNKS, MIX_TILE, COL_CHUNK), BF16),
        ],
        compiler_params=pltpu.CompilerParams(
            dimension_semantics=("arbitrary", "arbitrary"), vmem_limit_bytes=VMEM_LIMIT),
        name="mixer",
    )(*operands)

    tokens = batch * seq
    out = pl.pallas_call(
        _mlp_kernel,
        grid=(tokens // MLP_TILE,),
        in_specs=[pl.BlockSpec((MLP_TILE, D_MODEL), lambda i: (i, 0)),
                  _resident((1, D_MODEL)), _resident((D_MODEL, D_FF)), _resident((D_FF, D_MODEL))],
        out_specs=pl.BlockSpec((MLP_TILE, D_MODEL), lambda i: (i, 0)),
        out_shape=jax.ShapeDtypeStruct((tokens, D_MODEL), F32),
        scratch_shapes=[pltpu.VMEM((MLP_TILE, D_MODEL), BF16)],
        compiler_params=pltpu.CompilerParams(
            dimension_semantics=("arbitrary",), vmem_limit_bytes=VMEM_LIMIT),
        name="mlp",
    )(x1.reshape(tokens, D_MODEL), row(norm_mlp_g[layer]),
      w_ff1[layer].astype(BF16), w_ff2[layer].astype(BF16))
    return out.reshape(batch, seq, D_MODEL)
```

```python
import functools

import numpy as np
import jax
import jax.numpy as jnp
from jax import lax
from jax.experimental import pallas as pl
from jax.experimental.pallas import tpu as pltpu

D_MODEL = 1024
HEAD_DIM = 64
N_Q_HEADS = 16
N_KV_HEADS = 4
KV_WIDTH = N_KV_HEADS * HEAD_DIM
GROUP = N_Q_HEADS // N_KV_HEADS
BLOCK = 128
CONV_WIDTH = 31
D_FF = 4 * D_MODEL
N_BUCKETS = 32
MAX_DISTANCE = 128
EPS = 1e-6
NEG = -1e30
LOG2E = 1.4426950408889634
QK_BOUND_MARGIN = 1.03
MAX_EXP2_SPREAD = 100.0

LANES = 128
SUBLANES = 8
N_LANE_BLOCKS = D_MODEL // LANES
COL_CHUNK = 2 * LANES
N_COL_CHUNKS = D_MODEL // COL_CHUNK
CONV_HALO = 32
CONV_SKIP = CONV_HALO - (CONV_WIDTH - 1)
CONV_ROWS = 64

MIX_TILE = 512
MLP_TILE = 1024
FF_CHUNK = 1024
W_STAGE_BYTES = 1 << 20
VMEM_LIMIT = 56 * 1024 * 1024

F32 = jnp.float32
BF16 = jnp.bfloat16


def _bucket_table():
    qi = np.arange(BLOCK, dtype=np.int32)[:, None]
    kj = np.arange(2 * BLOCK, dtype=np.int32)[None, :]
    dist = qi + BLOCK - kj
    n = np.maximum(dist, 0)
    max_exact = N_BUCKETS // 2
    nf = np.maximum(n, 1).astype(np.float32)
    large = max_exact + (np.log(nf / np.float32(max_exact)) / np.float32(np.log(MAX_DISTANCE / max_exact))
                         * np.float32(N_BUCKETS - max_exact)).astype(np.int32)
    large = np.minimum(large, N_BUCKETS - 1)
    bucket = np.where(n < max_exact, n, large)
    valid = (dist >= 0) & (dist < BLOCK)
    return np.where(valid, bucket, -1).astype(np.int32)


def _head_mean_matrix():
    idx = np.arange(COL_CHUNK) // HEAD_DIM
    return (idx[:, None] == idx[None, :]).astype(np.float32) / HEAD_DIM


def _dot(a, b):
    return jnp.dot(a, b, preferred_element_type=F32)


def _dot_nt(a, b):
    return lax.dot_general(a, b, (((1,), (1,)), ((), ())), preferred_element_type=F32)


def _rms(x, gain):
    return x * lax.rsqrt(jnp.mean(x * x, axis=-1, keepdims=True) + EPS) * gain


def _head_rms(t, hm, gain):
    ms = _dot((t * t).astype(BF16), hm)
    return t * lax.rsqrt(ms + EPS) * gain


def _mixer_kernel(x_ref, bucket_ref, relb_ref, sink_ref, qk_bound_ref, hm_ref,
                  g_mix_ref, gq_ref, gk_ref, w_q_ref, w_kv_ref, w_a_ref, w_g_ref,
                  w_gates_ref, w_ao_ref, w_dw_ref, b_dw_ref, ln_g_ref, ln_b_ref,
                  w_co_ref, w_out_ref,
                  o_ref,
                  bias_s, sink_s, bounded_s, u_s, q_s, kdup_s, vdup_s, attn_s, h_s, conv_s, ga_s, gc_s):
    tile = x_ref.shape[1]
    n_blocks = tile // BLOCK
    b = pl.program_id(0)
    j = pl.program_id(1)

    @pl.when((b == 0) & (j == 0))
    def _build_bias():
        qk_bound = qk_bound_ref[0]
        his = [relb_ref[0, h] for h in range(N_Q_HEADS)]
        los = list(his)
        for bb in range(1, N_BUCKETS):
            his = [jnp.maximum(hi, relb_ref[bb, h]) for h, hi in enumerate(his)]
            los = [jnp.minimum(lo, relb_ref[bb, h]) for h, lo in enumerate(los)]
        spread = (his[0] - los[0]) * LOG2E
        for h in range(1, N_Q_HEADS):
            spread = jnp.maximum(spread, (his[h] - los[h]) * LOG2E)
        bounded = 2.0 * qk_bound + spread <= MAX_EXP2_SPREAD
        bounded_s[0] = bounded.astype(jnp.int32)

        bk = bucket_ref[...]
        left = lax.broadcasted_iota(jnp.int32, bk.shape, 1) < BLOCK
        for h in range(N_Q_HEADS):
            shift = jnp.where(bounded, qk_bound + his[h] * LOG2E, 0.0)
            acc = jnp.full(bk.shape, NEG, F32)
            for bb in range(N_BUCKETS):
                acc = jnp.where(bk == bb, relb_ref[bb, h] * LOG2E - shift, acc)
            bias_s[0, h] = acc
            bias_s[1, h] = jnp.where(left, NEG, acc)
            sink_s[h] = jnp.max(jnp.exp2(jnp.full((SUBLANES, LANES), sink_ref[h] * LOG2E - shift, F32)))

    @pl.when(j == 0)
    def _zero_halo():
        kdup_s[:, 0:BLOCK, :] = jnp.zeros((N_KV_HEADS, BLOCK, LANES), BF16)
        vdup_s[:, 0:BLOCK, :] = jnp.zeros((N_KV_HEADS, BLOCK, LANES), BF16)
        h_s[:, 0:CONV_HALO, :] = jnp.zeros((N_LANE_BLOCKS, CONV_HALO, LANES), F32)

    @pl.when(j != 0)
    def _carry_halo():
        kdup_s[:, 0:BLOCK, :] = kdup_s[:, tile:tile + BLOCK, :]
        vdup_s[:, 0:BLOCK, :] = vdup_s[:, tile:tile + BLOCK, :]
        h_s[:, 0:CONV_HALO, :] = h_s[:, tile:tile + CONV_HALO, :]

    x = x_ref[0]
    u = _rms(x, g_mix_ref[...]).astype(BF16)
    hm = hm_ref[...]

    def _conv_lane_block(c):
        for r in range(tile // CONV_ROWS):
            acc = jnp.broadcast_to(b_dw_ref[c], (CONV_ROWS, LANES))
            for a in range(SUBLANES):
                taps = range(a, CONV_WIDTH, SUBLANES)
                start = r * CONV_ROWS + CONV_SKIP + a
                rows = h_s[c, start:start + CONV_ROWS + SUBLANES * (len(taps) - 1), :]
                for k, t in enumerate(taps):
                    acc = acc + rows[SUBLANES * k:SUBLANES * k + CONV_ROWS] * w_dw_ref[c, t:t + 1, :]
            conv_s[c, r * CONV_ROWS:(r + 1) * CONV_ROWS, :] = acc

    def _gate_chunk(i, rows):
        gates = _dot(rows, w_gates_ref[i]).astype(BF16)
        ga_s[i] = gates[:, :COL_CHUNK]
        gc_s[i] = gates[:, COL_CHUNK:]

    hglu = _dot(u, w_a_ref[...]) * jax.nn.sigmoid(_dot(u, w_g_ref[...]))
    for c in range(N_LANE_BLOCKS):
        h_s[c, CONV_HALO:CONV_HALO + tile, :] = hglu[:, c * LANES:(c + 1) * LANES]

    kv = _dot(u, w_kv_ref[...])
    kn = _head_rms(kv[:, :KV_WIDTH], hm, gk_ref[...])
    vv = kv[:, KV_WIDTH:]
    low = lax.broadcasted_iota(jnp.int32, (tile, LANES), 1) < HEAD_DIM
    for src, dst in ((kn, kdup_s), (vv, vdup_s)):
        for c in range(KV_WIDTH // LANES):
            s0 = src[:, c * LANES:(c + 1) * LANES]
            s1 = pltpu.roll(s0, HEAD_DIM, 1)
            dst[2 * c + 0, BLOCK:BLOCK + tile, :] = jnp.where(low, s0, s1).astype(BF16)
            dst[2 * c + 1, BLOCK:BLOCK + tile, :] = jnp.where(low, s1, s0).astype(BF16)
    u_s[...] = u

    def _conv_and_q(i, carry):
        _conv_lane_block(i)
        qn = _head_rms(_dot(u_s[...], w_q_ref[i]), hm, gq_ref[i])
        even = (lax.broadcasted_iota(jnp.int32, qn.shape, 1) & HEAD_DIM) == 0
        q_s[0, i] = jnp.where(even, qn, 0.0).astype(BF16)
        q_s[1, i] = jnp.where(even, 0.0, qn).astype(BF16)
        return carry

    def _conv_and_gates(i, carry):
        _conv_lane_block(i + N_COL_CHUNKS)
        _gate_chunk(i, u_s[...])
        return carry

    lax.fori_loop(0, N_COL_CHUNKS, _conv_and_q, 0)
    lax.fori_loop(0, N_COL_CHUNKS, _conv_and_gates, 0)

    def _attn_block(bounded, n, carry):
        r0 = pl.multiple_of(n * BLOCK, BLOCK)
        first = jnp.where((j == 0) & (n == 0), 1, 0)
        low_b = lax.broadcasted_iota(jnp.int32, (BLOCK, LANES), 1) < HEAD_DIM
        for g in range(N_KV_HEADS):
            q4 = jnp.concatenate(
                [q_s[i, g, pl.ds(r0, BLOCK), pr * LANES:(pr + 1) * LANES] for pr in range(2) for i in range(2)],
                axis=0)
            kb = kdup_s[g, pl.ds(r0, 2 * BLOCK), :]
            vb = vdup_s[g, pl.ds(r0, 2 * BLOCK), :]
            s4 = _dot_nt(q4, kb)
            es, rs = [], []
            for hh in range(GROUP):
                h = GROUP * g + hh
                s = s4[hh * BLOCK:(hh + 1) * BLOCK] + bias_s[first, h]
                if bounded:
                    e = jnp.exp2(s)
                    denom = jnp.sum(e, axis=-1, keepdims=True) + sink_s[h]
                else:
                    m = jnp.max(s, axis=-1, keepdims=True)
                    e = jnp.exp2(s - m)
                    denom = jnp.sum(e, axis=-1, keepdims=True) + jnp.exp2(sink_ref[h] * LOG2E - m)
                es.append(e.astype(BF16))
                rs.append(1.0 / denom)
            o4 = _dot(jnp.concatenate(es, axis=0), vb)
            for pr in range(2):
                o_even = o4[(2 * pr) * BLOCK:(2 * pr + 1) * BLOCK] * rs[2 * pr]
                o_odd = o4[(2 * pr + 1) * BLOCK:(2 * pr + 2) * BLOCK] * rs[2 * pr + 1]
                p = 2 * g + pr
                attn_s[pl.ds(r0, BLOCK), p * LANES:(p + 1) * LANES] = jnp.where(low_b, o_even, o_odd).astype(BF16)
        return carry

    @pl.when(bounded_s[0] == 1)
    def _attention_constant_shift():
        lax.fori_loop(0, n_blocks, functools.partial(_attn_block, True), 0, unroll=True)

    @pl.when(bounded_s[0] != 1)
    def _attention_row_max():
        lax.fori_loop(0, n_blocks, functools.partial(_attn_block, False), 0)

    gate_a = jax.nn.sigmoid(jnp.concatenate([ga_s[i] for i in range(N_COL_CHUNKS)], axis=1).astype(F32))
    merged = gate_a * _dot(attn_s[...], w_ao_ref[...])

    cv = jnp.concatenate([conv_s[c] for c in range(N_LANE_BLOCKS)], axis=1)
    mu = jnp.mean(cv, axis=-1, keepdims=True)
    xc = cv - mu
    var = jnp.mean(xc * xc, axis=-1, keepdims=True)
    y = xc * lax.rsqrt(var + EPS) * ln_g_ref[...] + ln_b_ref[...]
    conv = _dot((y * jax.nn.sigmoid(y)).astype(BF16), w_co_ref[...])

    gate_c = jax.nn.sigmoid(jnp.concatenate([gc_s[i] for i in range(N_COL_CHUNKS)], axis=1).astype(F32))
    merged = merged + gate_c * conv
    o_ref[0] = x + _dot(merged.astype(BF16), w_out_ref[...])


def _stage_weight(w_hbm, w_s, stage_s, sem):
    rows = stage_s.shape[1]
    n_chunks = w_hbm.shape[0] // rows

    def copy(k):
        return pltpu.make_async_copy(w_hbm.at[pl.ds(k * rows, rows), :], stage_s.at[k % 2], sem.at[k % 2])

    copy(0).start()
    for k in range(n_chunks):
        if k + 1 < n_chunks:
            copy(k + 1).start()
        copy(k).wait()
        w_s[k * rows:(k + 1) * rows, :] = stage_s[k % 2].astype(BF16)


def _mlp_kernel(x_ref, g_ref, w1_hbm, w2_hbm, o_ref, xn_s, w1_s, w2_s, stage1_s, stage2_s, sem1, sem2):
    @pl.when(pl.program_id(0) == 0)
    def _load_weights():
        _stage_weight(w1_hbm, w1_s, stage1_s, sem1)
        _stage_weight(w2_hbm, w2_s, stage2_s, sem2)

    x = x_ref[...]
    xn_s[...] = _rms(x, g_ref[...]).astype(BF16)
    acc = x
    for c in range(D_FF // FF_CHUNK):
        cols = slice(c * FF_CHUNK, (c + 1) * FF_CHUNK)
        hmid = jnp.square(jnp.maximum(_dot(xn_s[...], w1_s[:, cols]), 0.0))
        acc = acc + _dot(hmid.astype(BF16), w2_s[cols, :])
    o_ref[...] = acc


def _resident(shape):
    zeros = (0,) * len(shape)
    return pl.BlockSpec(shape, lambda *_: zeros, pipeline_mode=pl.Buffered(1))


def _smem():
    return pl.BlockSpec(memory_space=pltpu.SMEM)


def kernel(x, norm_mix_g, w_in, q_norm_g, k_norm_g, attn_sinks, rel_bias, w_attn_o, w_dw, b_dw,
           conv_ln_g, conv_ln_b, w_conv_out, w_out, norm_mlp_g, w_ff1, w_ff2):
    batch, seq, d = x.shape
    assert d == D_MODEL and seq % MIX_TILE == 0 and (batch * seq) % MLP_TILE == 0
    assert MIX_TILE & (MIX_TILE - 1) == 0
    layer = 0
    row = lambda v: v.reshape(1, -1).astype(F32)
    w = w_in[layer].astype(BF16)
    q_end = N_Q_HEADS * HEAD_DIM
    k_end = q_end + KV_WIDTH
    v_end = k_end + KV_WIDTH
    a_end = v_end + D_MODEL
    g_end = a_end + D_MODEL
    ga_end = g_end + D_MODEL
    lane_blocks = lambda v: v.astype(F32).reshape(-1, N_LANE_BLOCKS, LANES).transpose(1, 0, 2)
    col_chunks = lambda m: m.reshape(m.shape[0], N_COL_CHUNKS, COL_CHUNK).transpose(1, 0, 2)
    q_gain = jnp.tile(q_norm_g[layer].astype(F32), N_Q_HEADS) * (HEAD_DIM ** -0.5 * LOG2E)
    qk_bound = (QK_BOUND_MARGIN * HEAD_DIM * jnp.max(jnp.abs(q_gain))
                * jnp.max(jnp.abs(k_norm_g[layer].astype(F32))))

    operands = [
        x,
        jnp.asarray(_bucket_table()),
        rel_bias.astype(F32),
        attn_sinks[layer].astype(F32),
        qk_bound.reshape(1),
        jnp.asarray(_head_mean_matrix(), dtype=BF16),
        row(norm_mix_g[layer]),
        q_gain.reshape(N_COL_CHUNKS, 1, COL_CHUNK),
        row(jnp.tile(k_norm_g[layer], N_KV_HEADS)),
        col_chunks(w[:, :q_end]), w[:, q_end:v_end], w[:, v_end:a_end], w[:, a_end:g_end],
        jnp.concatenate([col_chunks(w[:, g_end:ga_end]), col_chunks(w[:, ga_end:])], axis=2),
        w_attn_o[layer].astype(BF16),
        lane_blocks(w_dw[layer]),
        lane_blocks(b_dw[layer].reshape(1, -1)),
        row(conv_ln_g[layer]), row(conv_ln_b[layer]),
        w_conv_out[layer].astype(BF16),
        w_out[layer].astype(BF16),
    ]
    in_specs = [pl.BlockSpec((1, MIX_TILE, D_MODEL), lambda b, j: (b, j, 0)),
                _resident(operands[1].shape), _smem(), _smem(), _smem()]
    in_specs += [_resident(op.shape) for op in operands[5:]]

    x1 = pl.pallas_call(
        _mixer_kernel,
        grid=(batch, seq // MIX_TILE),
        in_specs=in_specs,
        out_specs=pl.BlockSpec((1, MIX_TILE, D_MODEL), lambda b, j: (b, j, 0)),
        out_shape=jax.ShapeDtypeStruct(x.shape, F32),
        scratch_shapes=[
            pltpu.VMEM((2, N_Q_HEADS, BLOCK, 2 * BLOCK), F32),
            pltpu.SMEM((N_Q_HEADS,), F32),
            pltpu.SMEM((1,), jnp.int32),
            pltpu.VMEM((MIX_TILE, D_MODEL), BF16),
            pltpu.VMEM((2, N_COL_CHUNKS, MIX_TILE, COL_CHUNK), BF16),
            pltpu.VMEM((N_KV_HEADS, BLOCK + MIX_TILE, LANES), BF16),
            pltpu.VMEM((N_KV_HEADS, BLOCK + MIX_TILE, LANES), BF16),
            pltpu.VMEM((MIX_TILE, D_MODEL), BF16),
            pltpu.VMEM((N_LANE_BLOCKS, CONV_HALO + MIX_TILE, LANES), F32),
            pltpu.VMEM((N_LANE_BLOCKS, MIX_TILE, LANES), F32),
            pltpu.VMEM((N_COL_CHUNKS, MIX_TILE, COL_CHUNK), BF16),
            pltpu.VMEM((N_COL_CHUNKS, MIX_TILE, COL_CHUNK), BF16),
        ],
        compiler_params=pltpu.CompilerParams(
            dimension_semantics=("arbitrary", "arbitrary"), vmem_limit_bytes=VMEM_LIMIT),
        name="mixer",
    )(*operands)

    tokens = batch * seq
    out = pl.pallas_call(
        _mlp_kernel,
        grid=(tokens // MLP_TILE,),
        in_specs=[pl.BlockSpec((MLP_TILE, D_MODEL), lambda i: (i, 0)),
                  _resident((1, D_MODEL)),
                  pl.BlockSpec(memory_space=pl.ANY), pl.BlockSpec(memory_space=pl.ANY)],
        out_specs=pl.BlockSpec((MLP_TILE, D_MODEL), lambda i: (i, 0)),
        out_shape=jax.ShapeDtypeStruct((tokens, D_MODEL), F32),
        scratch_shapes=[
            pltpu.VMEM((MLP_TILE, D_MODEL), BF16),
            pltpu.VMEM((D_MODEL, D_FF), BF16),
            pltpu.VMEM((D_FF, D_MODEL), BF16),
            pltpu.VMEM((2, W_STAGE_BYTES // (4 * D_FF), D_FF), F32),
            pltpu.VMEM((2, W_STAGE_BYTES // (4 * D_MODEL), D_MODEL), F32),
            pltpu.SemaphoreType.DMA((2,)),
            pltpu.SemaphoreType.DMA((2,)),
        ],
        compiler_params=pltpu.CompilerParams(
            dimension_semantics=("arbitrary",), vmem_limit_bytes=VMEM_LIMIT),
        name="mlp",
    )(x1.reshape(tokens, D_MODEL), row(norm_mlp_g[layer]),
      w_ff1[layer].astype(F32), w_ff2[layer].astype(F32))
    return out.reshape(batch, seq, D_MODEL)
```

```python
import functools

import numpy as np
import jax
import jax.numpy as jnp
from jax import lax
from jax.experimental import pallas as pl
from jax.experimental.pallas import tpu as pltpu

D_MODEL = 1024
HEAD_DIM = 64
N_Q_HEADS = 16
N_KV_HEADS = 4
KV_WIDTH = N_KV_HEADS * HEAD_DIM
GROUP = N_Q_HEADS // N_KV_HEADS
BLOCK = 128
CONV_WIDTH = 31
D_FF = 4 * D_MODEL
N_BUCKETS = 32
MAX_DISTANCE = 128
EPS = 1e-6
NEG = -1e30
LOG2E = 1.4426950408889634
QK_BOUND_MARGIN = 1.03
MAX_EXP2_SPREAD = 100.0

LANES = 128
SUBLANES = 8
N_LANE_BLOCKS = D_MODEL // LANES
COL_CHUNK = 2 * LANES
N_COL_CHUNKS = D_MODEL // COL_CHUNK
CONV_HALO = 32
CONV_SKIP = CONV_HALO - (CONV_WIDTH - 1)
CONV_ROWS = 64

MIX_TILE = 512
MLP_TILE = 1024
FF_CHUNK = 2048
VMEM_LIMIT = 56 * 1024 * 1024

F32 = jnp.float32
BF16 = jnp.bfloat16


def _bucket_table():
    qi = np.arange(BLOCK, dtype=np.int32)[:, None]
    kj = np.arange(2 * BLOCK, dtype=np.int32)[None, :]
    dist = qi + BLOCK - kj
    n = np.maximum(dist, 0)
    max_exact = N_BUCKETS // 2
    nf = np.maximum(n, 1).astype(np.float32)
    large = max_exact + (np.log(nf / np.float32(max_exact)) / np.float32(np.log(MAX_DISTANCE / max_exact))
                         * np.float32(N_BUCKETS - max_exact)).astype(np.int32)
    large = np.minimum(large, N_BUCKETS - 1)
    bucket = np.where(n < max_exact, n, large)
    valid = (dist >= 0) & (dist < BLOCK)
    return np.where(valid, bucket, -1).astype(np.int32)


def _head_mean_matrix():
    idx = np.arange(COL_CHUNK) // HEAD_DIM
    return (idx[:, None] == idx[None, :]).astype(np.float32) / HEAD_DIM


def _dot(a, b):
    return jnp.dot(a, b, preferred_element_type=F32)


def _dot_nt(a, b):
    return lax.dot_general(a, b, (((1,), (1,)), ((), ())), preferred_element_type=F32)


def _rms(x, gain):
    return x * lax.rsqrt(jnp.mean(x * x, axis=-1, keepdims=True) + EPS) * gain


def _head_rms(t, hm, gain):
    ms = _dot((t * t).astype(BF16), hm)
    return t * lax.rsqrt(ms + EPS) * gain


def _mixer_kernel(x_ref, bucket_ref, relb_ref, sink_ref, qk_bound_ref, hm_ref,
                  g_mix_ref, gq_ref, gk_ref, w_q_ref, w_kv_ref, w_a_ref, w_g_ref,
                  w_gates_ref, w_ao_ref, w_dw_ref, b_dw_ref, ln_g_ref, ln_b_ref,
                  w_co_ref, w_out_ref,
                  o_ref,
                  bias_s, sink_s, bounded_s, q_s, kdup_s, vdup_s, attn_s, h_s, conv_s, ga_s, gc_s):
    tile = x_ref.shape[1]
    n_blocks = tile // BLOCK
    b = pl.program_id(0)
    j = pl.program_id(1)

    @pl.when((b == 0) & (j == 0))
    def _build_bias():
        qk_bound = qk_bound_ref[0]
        his = [relb_ref[0, h] for h in range(N_Q_HEADS)]
        los = list(his)
        for bb in range(1, N_BUCKETS):
            his = [jnp.maximum(hi, relb_ref[bb, h]) for h, hi in enumerate(his)]
            los = [jnp.minimum(lo, relb_ref[bb, h]) for h, lo in enumerate(los)]
        spread = (his[0] - los[0]) * LOG2E
        for h in range(1, N_Q_HEADS):
            spread = jnp.maximum(spread, (his[h] - los[h]) * LOG2E)
        bounded = 2.0 * qk_bound + spread <= MAX_EXP2_SPREAD
        bounded_s[0] = bounded.astype(jnp.int32)

        bk = bucket_ref[...]
        left = lax.broadcasted_iota(jnp.int32, bk.shape, 1) < BLOCK
        for h in range(N_Q_HEADS):
            shift = jnp.where(bounded, qk_bound + his[h] * LOG2E, 0.0)
            acc = jnp.full(bk.shape, NEG, F32)
            for bb in range(N_BUCKETS):
                acc = jnp.where(bk == bb, relb_ref[bb, h] * LOG2E - shift, acc)
            bias_s[0, h] = acc
            bias_s[1, h] = jnp.where(left, NEG, acc)
            sink_s[h] = jnp.max(jnp.exp2(jnp.full((SUBLANES, LANES), sink_ref[h] * LOG2E - shift, F32)))

    @pl.when(j == 0)
    def _zero_halo():
        kdup_s[:, 0:BLOCK, :] = jnp.zeros((N_KV_HEADS, BLOCK, LANES), BF16)
        vdup_s[:, 0:BLOCK, :] = jnp.zeros((N_KV_HEADS, BLOCK, LANES), BF16)
        h_s[:, 0:CONV_HALO, :] = jnp.zeros((N_LANE_BLOCKS, CONV_HALO, LANES), F32)

    @pl.when(j != 0)
    def _carry_halo():
        kdup_s[:, 0:BLOCK, :] = kdup_s[:, tile:tile + BLOCK, :]
        vdup_s[:, 0:BLOCK, :] = vdup_s[:, tile:tile + BLOCK, :]
        h_s[:, 0:CONV_HALO, :] = h_s[:, tile:tile + CONV_HALO, :]

    x = x_ref[0]
    u = _rms(x, g_mix_ref[...]).astype(BF16)
    hm = hm_ref[...]

    def _conv_lane_block(c):
        for r in range(tile // CONV_ROWS):
            acc = jnp.broadcast_to(b_dw_ref[c], (CONV_ROWS, LANES))
            for a in range(SUBLANES):
                taps = range(a, CONV_WIDTH, SUBLANES)
                start = r * CONV_ROWS + CONV_SKIP + a
                rows = h_s[c, start:start + CONV_ROWS + SUBLANES * (len(taps) - 1), :]
                for k, t in enumerate(taps):
                    acc = acc + rows[SUBLANES * k:SUBLANES * k + CONV_ROWS] * w_dw_ref[c, t:t + 1, :]
            conv_s[c, r * CONV_ROWS:(r + 1) * CONV_ROWS, :] = acc

    def _gate_chunk(i, rows):
        gates = _dot(rows, w_gates_ref[i]).astype(BF16)
        ga_s[i] = gates[:, :COL_CHUNK]
        gc_s[i] = gates[:, COL_CHUNK:]

    hglu = _dot(u, w_a_ref[...]) * jax.nn.sigmoid(_dot(u, w_g_ref[...]))
    for c in range(N_LANE_BLOCKS):
        h_s[c, CONV_HALO:CONV_HALO + tile, :] = hglu[:, c * LANES:(c + 1) * LANES]

    kv = _dot(u, w_kv_ref[...])
    kn = _head_rms(kv[:, :KV_WIDTH], hm, gk_ref[...])
    vv = kv[:, KV_WIDTH:]
    low = lax.broadcasted_iota(jnp.int32, (tile, LANES), 1) < HEAD_DIM
    for src, dst in ((kn, kdup_s), (vv, vdup_s)):
        for c in range(KV_WIDTH // LANES):
            s0 = src[:, c * LANES:(c + 1) * LANES]
            s1 = pltpu.roll(s0, HEAD_DIM, 1)
            dst[2 * c + 0, BLOCK:BLOCK + tile, :] = jnp.where(low, s0, s1).astype(BF16)
            dst[2 * c + 1, BLOCK:BLOCK + tile, :] = jnp.where(low, s1, s0).astype(BF16)

    def _conv_and_q(i, carry):
        _conv_lane_block(i)
        qn = _head_rms(_dot(u, w_q_ref[i]), hm, gq_ref[i])
        even = (lax.broadcasted_iota(jnp.int32, qn.shape, 1) & HEAD_DIM) == 0
        q_s[0, i] = jnp.where(even, qn, 0.0).astype(BF16)
        q_s[1, i] = jnp.where(even, 0.0, qn).astype(BF16)
        return carry

    def _conv_and_gates(i, carry):
        _conv_lane_block(i + N_COL_CHUNKS)
        _gate_chunk(i, u)
        return carry

    lax.fori_loop(0, N_COL_CHUNKS, _conv_and_q, 0)
    lax.fori_loop(0, N_COL_CHUNKS, _conv_and_gates, 0)

    def _attn_block(bounded, n, carry):
        r0 = pl.multiple_of(n * BLOCK, BLOCK)
        first = jnp.where((j == 0) & (n == 0), 1, 0)
        low_b = lax.broadcasted_iota(jnp.int32, (BLOCK, LANES), 1) < HEAD_DIM
        for g in range(N_KV_HEADS):
            q4 = jnp.concatenate(
                [q_s[i, g, pl.ds(r0, BLOCK), pr * LANES:(pr + 1) * LANES] for pr in range(2) for i in range(2)],
                axis=0)
            kb = kdup_s[g, pl.ds(r0, 2 * BLOCK), :]
            vb = vdup_s[g, pl.ds(r0, 2 * BLOCK), :]
            s4 = _dot_nt(q4, kb)
            es, rs = [], []
            for hh in range(GROUP):
                h = GROUP * g + hh
                s = s4[hh * BLOCK:(hh + 1) * BLOCK] + bias_s[first, h]
                if bounded:
                    e = jnp.exp2(s)
                    denom = jnp.sum(e, axis=-1, keepdims=True) + sink_s[h]
                else:
                    m = jnp.max(s, axis=-1, keepdims=True)
                    e = jnp.exp2(s - m)
                    denom = jnp.sum(e, axis=-1, keepdims=True) + jnp.exp2(sink_ref[h] * LOG2E - m)
                es.append(e.astype(BF16))
                rs.append(1.0 / denom)
            o4 = _dot(jnp.concatenate(es, axis=0), vb)
            for pr in range(2):
                o_even = o4[(2 * pr) * BLOCK:(2 * pr + 1) * BLOCK] * rs[2 * pr]
                o_odd = o4[(2 * pr + 1) * BLOCK:(2 * pr + 2) * BLOCK] * rs[2 * pr + 1]
                p = 2 * g + pr
                attn_s[pl.ds(r0, BLOCK), p * LANES:(p + 1) * LANES] = jnp.where(low_b, o_even, o_odd).astype(BF16)
        return carry

    @pl.when(bounded_s[0] == 1)
    def _attention_constant_shift():
        lax.fori_loop(0, n_blocks, functools.partial(_attn_block, True), 0, unroll=True)

    @pl.when(bounded_s[0] != 1)
    def _attention_row_max():
        lax.fori_loop(0, n_blocks, functools.partial(_attn_block, False), 0)

    gate_a = jax.nn.sigmoid(jnp.concatenate([ga_s[i] for i in range(N_COL_CHUNKS)], axis=1).astype(F32))
    merged = gate_a * _dot(attn_s[...], w_ao_ref[...])

    cv = jnp.concatenate([conv_s[c] for c in range(N_LANE_BLOCKS)], axis=1)
    mu = jnp.mean(cv, axis=-1, keepdims=True)
    xc = cv - mu
    var = jnp.mean(xc * xc, axis=-1, keepdims=True)
    y = xc * lax.rsqrt(var + EPS) * ln_g_ref[...] + ln_b_ref[...]
    conv = _dot((y * jax.nn.sigmoid(y)).astype(BF16), w_co_ref[...])

    gate_c = jax.nn.sigmoid(jnp.concatenate([gc_s[i] for i in range(N_COL_CHUNKS)], axis=1).astype(F32))
    merged = merged + gate_c * conv
    o_ref[0] = x + _dot(merged.astype(BF16), w_out_ref[...])


def _mlp_kernel(x_ref, g_ref, w1_ref, w2_ref, o_ref, xn_s):
    x = x_ref[...]
    xn_s[...] = _rms(x, g_ref[...]).astype(BF16)
    acc = x
    for c in range(D_FF // FF_CHUNK):
        cols = slice(c * FF_CHUNK, (c + 1) * FF_CHUNK)
        hmid = jnp.square(jnp.maximum(_dot(xn_s[...], w1_ref[:, cols]), 0.0))
        acc = acc + _dot(hmid.astype(BF16), w2_ref[cols, :])
    o_ref[...] = acc


def _resident(shape):
    zeros = (0,) * len(shape)
    return pl.BlockSpec(shape, lambda *_: zeros, pipeline_mode=pl.Buffered(1))


def _smem():
    return pl.BlockSpec(memory_space=pltpu.SMEM)


def kernel(x, norm_mix_g, w_in, q_norm_g, k_norm_g, attn_sinks, rel_bias, w_attn_o, w_dw, b_dw,
           conv_ln_g, conv_ln_b, w_conv_out, w_out, norm_mlp_g, w_ff1, w_ff2):
    batch, seq, d = x.shape
    assert d == D_MODEL and seq % MIX_TILE == 0 and (batch * seq) % MLP_TILE == 0
    assert MIX_TILE & (MIX_TILE - 1) == 0
    layer = 0
    row = lambda v: v.reshape(1, -1).astype(F32)
    w = w_in[layer].astype(BF16)
    q_end = N_Q_HEADS * HEAD_DIM
    k_end = q_end + KV_WIDTH
    v_end = k_end + KV_WIDTH
    a_end = v_end + D_MODEL
    g_end = a_end + D_MODEL
    ga_end = g_end + D_MODEL
    lane_blocks = lambda v: v.astype(F32).reshape(-1, N_LANE_BLOCKS, LANES).transpose(1, 0, 2)
    col_chunks = lambda m: m.reshape(m.shape[0], N_COL_CHUNKS, COL_CHUNK).transpose(1, 0, 2)
    q_gain = jnp.tile(q_norm_g[layer].astype(F32), N_Q_HEADS) * (HEAD_DIM ** -0.5 * LOG2E)
    qk_bound = (QK_BOUND_MARGIN * HEAD_DIM * jnp.max(jnp.abs(q_gain))
                * jnp.max(jnp.abs(k_norm_g[layer].astype(F32))))

    operands = [
        x,
        jnp.asarray(_bucket_table()),
        rel_bias.astype(F32),
        attn_sinks[layer].astype(F32),
        qk_bound.reshape(1),
        jnp.asarray(_head_mean_matrix(), dtype=BF16),
        row(norm_mix_g[layer]),
        q_gain.reshape(N_COL_CHUNKS, 1, COL_CHUNK),
        row(jnp.tile(k_norm_g[layer], N_KV_HEADS)),
        col_chunks(w[:, :q_end]), w[:, q_end:v_end], w[:, v_end:a_end], w[:, a_end:g_end],
        jnp.concatenate([col_chunks(w[:, g_end:ga_end]), col_chunks(w[:, ga_end:])], axis=2),
        w_attn_o[layer].astype(BF16),
        lane_blocks(w_dw[layer]),
        lane_blocks(b_dw[layer].reshape(1, -1)),
        row(conv_ln_g[layer]), row(conv_ln_b[layer]),
        w_conv_out[layer].astype(BF16),
        w_out[layer].astype(BF16),
    ]
    in_specs = [pl.BlockSpec((1, MIX_TILE, D_MODEL), lambda b, j: (b, j, 0)),
                _resident(operands[1].shape), _smem(), _smem(), _smem()]
    in_specs += [_resident(op.shape) for op in operands[5:]]

    x1 = pl.pallas_call(
        _mixer_kernel,
        grid=(batch, seq // MIX_TILE),
        in_specs=in_specs,
        out_specs=pl.BlockSpec((1, MIX_TILE, D_MODEL), lambda b, j: (b, j, 0)),
        out_shape=jax.ShapeDtypeStruct(x.shape, F32),
        scratch_shapes=[
            pltpu.VMEM((2, N_Q_HEADS, BLOCK, 2 * BLOCK), F32),
            pltpu.SMEM((N_Q_HEADS,), F32),
            pltpu.SMEM((1,), jnp.int32),
            pltpu.VMEM((2, N_COL_CHUNKS, MIX_TILE, COL_CHUNK), BF16),
            pltpu.VMEM((N_KV_HEADS, BLOCK + MIX_TILE, LANES), BF16),
            pltpu.VMEM((N_KV_HEADS, BLOCK + MIX_TILE, LANES), BF16),
            pltpu.VMEM((MIX_TILE, D_MODEL), BF16),
            pltpu.VMEM((N_LANE_BLOCKS, CONV_HALO + MIX_TILE, LANES), F32),
            pltpu.VMEM((N_LANE_BLOCKS, MIX_TILE, LANES), F32),
            pltpu.VMEM((N_COL_CHUNKS, MIX_TILE, COL_CHUNK), BF16),
            pltpu.VMEM((N_COL_CHUNKS, MIX_TILE, COL_CHUNK), BF16),
        ],
        compiler_params=pltpu.CompilerParams(
            dimension_semantics=("arbitrary", "arbitrary"), vmem_limit_bytes=VMEM_LIMIT),
        name="mixer",
    )(*operands)

    tokens = batch * seq
    out = pl.pallas_call(
        _mlp_kernel,
        grid=(tokens // MLP_TILE,),
        in_specs=[pl.BlockSpec((MLP_TILE, D_MODEL), lambda i: (i, 0)),
                  _resident((1, D_MODEL)), _resident((D_MODEL, D_FF)), _resident((D_FF, D_MODEL))],
        out_specs=pl.BlockSpec((MLP_TILE, D_MODEL), lambda i: (i, 0)),
        out_shape=jax.ShapeDtypeStruct((tokens, D_MODEL), F32),
        scratch_shapes=[pltpu.VMEM((MLP_TILE, D_MODEL), BF16)],
        compiler_params=pltpu.CompilerParams(
            dimension_semantics=("arbitrary",), vmem_limit_bytes=VMEM_LIMIT),
        name="mlp",
    )(x1.reshape(tokens, D_MODEL), row(norm_mlp_g[layer]),
      w_ff1[layer].astype(BF16), w_ff2[layer].astype(BF16))
    return out.reshape(batch, seq, D_MODEL)
```
